```python
import jax, jax.numpy as jnp
from jax import lax
import numpy as np

D_MODEL = 1024
BATCH = 4
SEQ = 8192
DEPTH = 2

N_MIXERS = 2
CONV_WIDTH = 3
ATTN_GROUPS = ((128, 1), (512, 4), (2048, 16))
N_GROUPS = len(ATTN_GROUPS)
HEADS_PER_GROUP = 5
HEAD_DIM = 64
ATTN_WIDTH = N_GROUPS * HEADS_PER_GROUP * HEAD_DIM
ROPE_THETA = 10000.0
D_FF = 2816
N_EXPERTS = 8
TOP_K = 2
D_FF_EXPERT = 2816
RMS_EPS = 1e-6
N_EVEN = (DEPTH + 1) // 2
N_ODD = DEPTH // 2

kernel_name = "hybrid_conv_dilated_attn_moe_adaln"


def rms_norm(x, g):
    xf = x.astype(jnp.float32)
    y = xf * lax.rsqrt(jnp.mean(xf * xf, axis=-1, keepdims=True) + RMS_EPS)
    return (y * g.astype(jnp.float32)).astype(x.dtype)


def ada_params(c, w, b):
    mod = jax.nn.silu(c) @ w + b
    shift, scale, gate = jnp.split(mod, 3, axis=-1)
    return shift[:, None, :], scale[:, None, :], gate[:, None, :]


def modulated_residual(x, c, g, w_mod, b_mod, fn):
    shift, scale, gate = ada_params(c, w_mod, b_mod)
    h = rms_norm(x, g) * (1 + scale) + shift
    return x + gate * fn(h)


def rope(x, positions):
    inv_freq = ROPE_THETA ** (-jnp.arange(0, HEAD_DIM, 2, dtype=jnp.float32) / HEAD_DIM)
    ang = positions.astype(jnp.float32)[..., None] * inv_freq
    cos, sin = jnp.cos(ang)[:, :, None, :], jnp.sin(ang)[:, :, None, :]
    xf = x.astype(jnp.float32)
    x1, x2 = jnp.split(xf, 2, axis=-1)
    return jnp.concatenate([x1 * cos - x2 * sin, x2 * cos + x1 * sin], axis=-1).astype(x.dtype)


def short_conv_mixer(h, w_in, conv_w, w_out):
    b_gate, c_gate, u = jnp.split(h @ w_in, 3, axis=-1)
    v = c_gate * u
    conv = lax.conv_general_dilated(
        v, conv_w.astype(v.dtype)[:, None, :], window_strides=(1,),
        padding=[(CONV_WIDTH - 1, 0)], dimension_numbers=('NWC', 'WIO', 'NWC'),
        feature_group_count=v.shape[-1])
    return (b_gate * conv) @ w_out


def dilated_group_attention(q, k, v, dilation, steps):
    b, s, nh, dh = q.shape
    chunk = dilation * steps
    sp = -(-s // chunk) * chunk
    nb = sp // chunk
    pad = [(0, 0), (0, sp - s), (0, 0), (0, 0)]

    def to_blocks(t):
        return jnp.pad(t, pad).reshape(b, nb, steps, dilation, nh, dh)

    def with_prev(t):
        prev = jnp.pad(t[:, :-1], [(0, 0), (1, 0), (0, 0), (0, 0), (0, 0), (0, 0)])
        return jnp.concatenate([prev, t], axis=2)

    qb = to_blocks(q)
    kk = with_prev(to_blocks(k))
    vv = with_prev(to_blocks(v))
    scores = jnp.einsum('bnqrhd,bnkrhd->bnrhqk', qb, kk,
                        preferred_element_type=jnp.float32) * (dh ** -0.5)
    a = jnp.arange(steps)[:, None]
    j = jnp.arange(2 * steps)[None, :]
    band = (j >= a) & (j <= a + steps)
    not_first = jnp.arange(nb)[:, None, None] > 0
    valid = band[None] & (not_first | (j >= steps)[None])
    scores = jnp.where(valid[None, :, None, None], scores, -jnp.inf)
    lse = jax.nn.logsumexp(scores, axis=-1)
    p = jnp.exp(scores - lse[..., None])
    o = jnp.einsum('bnrhqk,bnkrhd->bnqrhd', p.astype(v.dtype), vv)
    o = o.reshape(b, sp, nh, dh)[:, :s]
    lse = lse.transpose(0, 1, 4, 2, 3).reshape(b, sp, nh)[:, :s]
    return o, lse


def dilated_attention_mixer(h, positions, w_qkv, w_o):
    b, s, _ = h.shape
    nh = N_GROUPS * HEADS_PER_GROUP
    qkv = (h @ w_qkv).reshape(b, s, 3, nh, HEAD_DIM)
    q = rope(qkv[:, :, 0], positions).reshape(b, s, N_GROUPS, HEADS_PER_GROUP, HEAD_DIM)
    k = rope(qkv[:, :, 1], positions).reshape(b, s, N_GROUPS, HEADS_PER_GROUP, HEAD_DIM)
    v = qkv[:, :, 2].reshape(b, s, N_GROUPS, HEADS_PER_GROUP, HEAD_DIM)
    outs, lses = [], []
    for g, (window, dil) in enumerate(ATTN_GROUPS):
        o, l = dilated_group_attention(q[:, :, g], k[:, :, g], v[:, :, g], dil, window // dil)
        outs.append(o)
        lses.append(l)
    alpha = jax.nn.softmax(jnp.stack(lses, axis=2), axis=2)
    o = jnp.stack(outs, axis=2) * alpha[..., None].astype(h.dtype)
    return o.reshape(b, s, ATTN_WIDTH) @ w_o


def swiglu(h, w_gate, w_up, w_down):
    return (jax.nn.silu(h @ w_gate) * (h @ w_up)) @ w_down


def moe_swiglu(h, w_router, w_gate, w_up, w_down):
    logits = jnp.einsum('bsd,de->bse', h, w_router, preferred_element_type=jnp.float32)
    top_val, top_idx = lax.top_k(logits, TOP_K)
    top_w = jax.nn.softmax(top_val, axis=-1)
    gates = jnp.sum(jax.nn.one_hot(top_idx, N_EXPERTS, dtype=jnp.float32) * top_w[..., None],
                    axis=-2).astype(h.dtype)
    out = jnp.zeros_like(h)
    for e in range(N_EXPERTS):
        out = out + gates[..., e:e + 1] * swiglu(h, w_gate[e], w_up[e], w_down[e])
    return out


def setup_inputs(seed: int = 0) -> dict:
    key = jax.random.key(seed)
    ks = jax.random.split(key, 20)
    D, F, FE, E = D_MODEL, D_FF, D_FF_EXPERT, N_EXPERTS

    def w(k, shape, fan_in, scale=1.0):
        return jax.random.normal(k, shape, jnp.float32) * (scale * fan_in ** -0.5)

    offsets = jax.random.randint(ks[2], (BATCH, 1), 0, 1024, dtype=jnp.int32)
    return {
        "x": jax.random.normal(ks[0], (BATCH, SEQ, D), jnp.float32),
        "c": jax.random.normal(ks[1], (BATCH, D), jnp.float32),
        "positions": offsets + jnp.arange(SEQ, dtype=jnp.int32)[None, :],
        "mod_w": w(ks[3], (DEPTH, 2, D, 3 * D), D, 0.5),
        "mod_b": 0.02 * jax.random.normal(ks[4], (DEPTH, 2, 3 * D), jnp.float32),
        "norm_g": 1.0 + 0.05 * jax.random.normal(ks[5], (DEPTH, 2, D), jnp.float32),
        "conv_w_in": w(ks[6], (N_EVEN, D, 3 * D), D),
        "conv_w": w(ks[7], (N_EVEN, CONV_WIDTH, D), CONV_WIDTH),
        "conv_w_out": w(ks[8], (N_EVEN, D, D), D),
        "ffn_w_gate": w(ks[9], (N_EVEN, D, F), D),
        "ffn_w_up": w(ks[10], (N_EVEN, D, F), D),
        "ffn_w_down": w(ks[11], (N_EVEN, F, D), F),
        "attn_w_qkv": w(ks[12], (N_ODD, D, 3 * ATTN_WIDTH), D),
        "attn_w_o": w(ks[13], (N_ODD, ATTN_WIDTH, D), ATTN_WIDTH),
        "router_w": w(ks[14], (N_ODD, D, E), D),
        "moe_w_gate": w(ks[15], (N_ODD, E, D, FE), D),
        "moe_w_up": w(ks[16], (N_ODD, E, D, FE), D),
        "moe_w_down": w(ks[17], (N_ODD, E, FE, D), FE),
        "final_g": 1.0 + 0.05 * jax.random.normal(ks[18], (D,), jnp.float32),
    }


def reference(x, c, positions, mod_w, mod_b, norm_g, conv_w_in, conv_w, conv_w_out,
              ffn_w_gate, ffn_w_up, ffn_w_down, attn_w_qkv, attn_w_o, router_w,
              moe_w_gate, moe_w_up, moe_w_down, final_g):
    c = c.astype(x.dtype)
    for i in range(DEPTH):
        j = i // N_MIXERS
        if i % N_MIXERS == 0:
            mixer = lambda h, j=j: short_conv_mixer(h, conv_w_in[j], conv_w[j], conv_w_out[j])
            channel = lambda h, j=j: swiglu(h, ffn_w_gate[j], ffn_w_up[j], ffn_w_down[j])
        else:
            mixer = lambda h, j=j: dilated_attention_mixer(h, positions, attn_w_qkv[j], attn_w_o[j])
            channel = lambda h, j=j: moe_swiglu(h, router_w[j], moe_w_gate[j], moe_w_up[j],
                                                moe_w_down[j])
        x = modulated_residual(x, c, norm_g[i, 0], mod_w[i, 0], mod_b[i, 0], mixer)
        x = modulated_residual(x, c, norm_g[i, 1], mod_w[i, 1], mod_b[i, 1], channel)
    return rms_norm(x, final_g)
```

```python
import functools
import math

import jax
import jax.numpy as jnp
from jax import lax
from jax.experimental import pallas as pl
from jax.experimental.pallas import tpu as pltpu

CONV_WIDTH = 3
ATTN_GROUPS = ((128, 1), (512, 4), (2048, 16))
HEADS_PER_GROUP = 5
HEAD_DIM = 64
ROPE_THETA = 10000.0
TOP_K = 2
RMS_EPS = 1e-6

LANES = 128
SUBLANES = 8
VMEM_LIMIT_BYTES = 56 * 1024 * 1024

GROUP_LANES = 384
PAIRS_PER_GROUP = GROUP_LANES // LANES
ATTN_STEPS = 128
ROW_SLAB = 8

F32 = jnp.float32
BF16 = jnp.bfloat16
NEG_INF = float("-inf")


def _cparams(semantics):
    return pltpu.CompilerParams(dimension_semantics=semantics, vmem_limit_bytes=VMEM_LIMIT_BYTES)


def _norm_mod(x, g, scale, shift):
    y = x * lax.rsqrt(jnp.mean(x * x, axis=-1, keepdims=True) + RMS_EPS)
    return (y * g) * (1.0 + scale) + shift


def _split_mod(mod_ref, b, d):
    row = mod_ref[0, pl.ds(b, 1), :]
    return row[:, :d], row[:, d:2 * d], row[:, 2 * d:]


def _silu(x):
    return x * jax.nn.sigmoid(x)


def _ada_kernel(c_ref, w_ref, b_ref, o_ref):
    s = _silu(c_ref[...])
    o_ref[0] = jnp.dot(s, w_ref[0], precision=lax.Precision.HIGHEST,
                       preferred_element_type=F32) + b_ref[0]


def _ada_params(c_pad, mod_w, mod_b):
    n_sets, d, d3 = mod_w.shape
    tn = d
    return pl.pallas_call(
        _ada_kernel,
        out_shape=jax.ShapeDtypeStruct((n_sets, SUBLANES, d3), F32),
        grid=(n_sets, d3 // tn),
        in_specs=[
            pl.BlockSpec((SUBLANES, d), lambda s, j: (0, 0)),
            pl.BlockSpec((1, d, tn), lambda s, j: (s, 0, j)),
            pl.BlockSpec((1, 1, tn), lambda s, j: (s, 0, j)),
        ],
        out_specs=pl.BlockSpec((1, SUBLANES, tn), lambda s, j: (s, 0, j)),
        compiler_params=_cparams(("parallel", "parallel")),
        name="ada_params",
    )(c_pad, mod_w, mod_b)


def _conv_mixer_kernel(x_ref, mod_ref, g_ref, win_ref, cw_ref, wout_ref, o_ref, carry_ref):
    b, i = pl.program_id(0), pl.program_id(1)
    tm, d = x_ref.shape[1], x_ref.shape[2]

    @pl.when(i == 0)
    def _():
        carry_ref[...] = jnp.zeros_like(carry_ref)

    x = x_ref[0]
    shift, scale, gate = _split_mod(mod_ref, b, d)
    h = _norm_mod(x, g_ref[...], scale, shift).astype(BF16)
    bcu = jnp.dot(h, win_ref[...], preferred_element_type=F32)
    b_gate, v = bcu[:, :d], bcu[:, d:2 * d] * bcu[:, 2 * d:]
    row = lax.broadcasted_iota(jnp.int32, (tm, 1), 0)
    prev = carry_ref[...]
    v1 = jnp.where(row == 0, prev[7:8], pltpu.roll(v, 1, 0))
    v2 = jnp.where(row == 0, prev[6:7], jnp.where(row == 1, prev[7:8], pltpu.roll(v, 2, 0)))
    cw = cw_ref[...]
    conv = cw[0:1] * v2 + cw[1:2] * v1 + cw[2:3] * v
    carry_ref[...] = v[tm - SUBLANES:, :]
    y = jnp.dot((b_gate * conv).astype(BF16), wout_ref[...], preferred_element_type=F32)
    o_ref[0] = x + gate * y


def _conv_mixer(x, mods, set_idx, g, w_in, conv_w, w_out, tm):
    bsz, s, d = x.shape
    return pl.pallas_call(
        _conv_mixer_kernel,
        out_shape=jax.ShapeDtypeStruct(x.shape, F32),
        grid=(bsz, s // tm),
        in_specs=[
            pl.BlockSpec((1, tm, d), lambda b, i: (b, i, 0)),
            pl.BlockSpec((1, SUBLANES, 3 * d), lambda b, i: (set_idx, 0, 0)),
            pl.BlockSpec((1, d), lambda b, i: (0, 0)),
            pl.BlockSpec((d, 3 * d), lambda b, i: (0, 0)),
            pl.BlockSpec((CONV_WIDTH, d), lambda b, i: (0, 0)),
            pl.BlockSpec((d, d), lambda b, i: (0, 0)),
        ],
        out_specs=pl.BlockSpec((1, tm, d), lambda b, i: (b, i, 0)),
        scratch_shapes=[pltpu.VMEM((SUBLANES, d), F32)],
        compiler_params=_cparams(("arbitrary", "arbitrary")),
        name="conv_mixer",
    )(x, mods, g, w_in, conv_w, w_out)


def _ffn_kernel(x_ref, mod_ref, g_ref, wg_ref, wu_ref, wd_ref, o_ref):
    b = pl.program_id(0)
    d = x_ref.shape[2]
    x = x_ref[0]
    shift, scale, gate = _split_mod(mod_ref, b, d)
    h = _norm_mod(x, g_ref[...], scale, shift).astype(BF16)
    a = _silu(jnp.dot(h, wg_ref[...], preferred_element_type=F32)) * jnp.dot(
        h, wu_ref[...], preferred_element_type=F32)
    y = jnp.dot(a.astype(BF16), wd_ref[...], preferred_element_type=F32)
    o_ref[0] = x + gate * y


def _dense_ffn(x, mods, set_idx, g, w_gate, w_up, w_down, tm):
    bsz, s, d = x.shape
    f = w_gate.shape[1]
    resident = pl.Buffered(1)
    return pl.pallas_call(
        _ffn_kernel,
        out_shape=jax.ShapeDtypeStruct(x.shape, F32),
        grid=(bsz, s // tm),
        in_specs=[
            pl.BlockSpec((1, tm, d), lambda b, i: (b, i, 0)),
            pl.BlockSpec((1, SUBLANES, 3 * d), lambda b, i: (set_idx, 0, 0)),
            pl.BlockSpec((1, d), lambda b, i: (0, 0)),
            pl.BlockSpec((d, f), lambda b, i: (0, 0), pipeline_mode=resident),
            pl.BlockSpec((d, f), lambda b, i: (0, 0), pipeline_mode=resident),
            pl.BlockSpec((f, d), lambda b, i: (0, 0), pipeline_mode=resident),
        ],
        out_specs=pl.BlockSpec((1, tm, d), lambda b, i: (b, i, 0)),
        compiler_params=_cparams(("parallel", "parallel")),
        name="dense_ffn",
    )(x, mods, g, w_gate, w_up, w_down)


def _class_major(ref, tm, r):
    n_chunks = ref.shape[0]
    if r == 1:
        return jnp.concatenate([ref[c] for c in range(n_chunks)], axis=1)
    return jnp.concatenate(
        [jnp.concatenate([ref[c, pl.ds(res, tm // r, stride=r), :] for c in range(n_chunks)], axis=1)
         for res in range(r)], axis=0)


def _store_lane_chunks(ref, val):
    for c in range(ref.shape[0]):
        ref[c] = val[:, c * LANES:(c + 1) * LANES]


def _rope(t, cos, sin_signed, first_half):
    out = []
    for c in range(t.shape[1] // LANES):
        tc = t[:, c * LANES:(c + 1) * LANES]
        rot = jnp.where(first_half, pltpu.roll(tc, LANES - HEAD_DIM // 2, 1),
                        pltpu.roll(tc, HEAD_DIM // 2, 1))
        out.append(tc * cos + rot * sin_signed)
    return jnp.concatenate(out, axis=1)


def _qkv_kernel(x_ref, mod_ref, g_ref, pos_ref, freq_ref, w_ref, o0_ref, o1_ref, o2_ref,
                h_scr, cos_scr, sin_scr):
    b = pl.program_id(0)
    tm, d = x_ref.shape[1], x_ref.shape[2]
    shift, scale, _ = _split_mod(mod_ref, b, d)
    _store_lane_chunks(h_scr, _norm_mod(x_ref[0], g_ref[...], scale, shift))
    ang = pos_ref[0].astype(F32) * freq_ref[...]
    first_half = (lax.broadcasted_iota(jnp.int32, (1, LANES), 1) % HEAD_DIM) < HEAD_DIM // 2
    cos_scr[0] = jnp.cos(ang)
    sin_scr[0] = jnp.where(first_half, -jnp.sin(ang), jnp.sin(ang))
    gl = GROUP_LANES
    for gi, ((_, r), o_ref) in enumerate(zip(ATTN_GROUPS, (o0_ref, o1_ref, o2_ref))):
        hg = _class_major(h_scr, tm, r).astype(BF16)
        cos, sin_signed = _class_major(cos_scr, tm, r), _class_major(sin_scr, tm, r)
        qkv = jnp.dot(hg, w_ref[gi], preferred_element_type=F32)
        q = _rope(qkv[:, :gl], cos, sin_signed, first_half) * (HEAD_DIM ** -0.5)
        k = _rope(qkv[:, gl:2 * gl], cos, sin_signed, first_half)
        o_ref[0, :, :gl] = q.astype(BF16)
        o_ref[0, :, gl:2 * gl] = k.astype(BF16)
        o_ref[0, :, 2 * gl:] = qkv[:, 2 * gl:].astype(BF16)


def _qkv_rope(x, mods, set_idx, g, pos3, freq, w_groups, tm):
    bsz, s, d = x.shape
    n_groups = len(ATTN_GROUPS)
    out = jax.ShapeDtypeStruct((bsz, s, 3 * GROUP_LANES), BF16)
    return pl.pallas_call(
        _qkv_kernel,
        out_shape=[out] * n_groups,
        grid=(bsz, s // tm),
        in_specs=[
            pl.BlockSpec((1, tm, d), lambda b, i: (b, i, 0)),
            pl.BlockSpec((1, SUBLANES, 3 * d), lambda b, i: (set_idx, 0, 0)),
            pl.BlockSpec((1, d), lambda b, i: (0, 0)),
            pl.BlockSpec((1, tm, 1), lambda b, i: (b, i, 0)),
            pl.BlockSpec((1, LANES), lambda b, i: (0, 0)),
            pl.BlockSpec((n_groups, d, 3 * GROUP_LANES), lambda b, i: (0, 0, 0)),
        ],
        out_specs=[pl.BlockSpec((1, tm, 3 * GROUP_LANES), lambda b, i: (b, i, 0))] * n_groups,
        scratch_shapes=[pltpu.VMEM((d // LANES, tm, LANES), F32), pltpu.VMEM((1, tm, LANES), F32),
                        pltpu.VMEM((1, tm, LANES), F32)],
        compiler_params=_cparams(("parallel", "parallel")),
        name="qkv_rope",
    )(x, mods, g, pos3, freq, w_groups)


def _store_rows(ref, a, val):
    st, per_tile = ATTN_STEPS, ref.shape[1]
    if per_tile >= st:
        ref[(a * st) // per_tile, pl.ds((a * st) % per_tile, st), :] = val
    else:
        for j in range(st // per_tile):
            ref[a * (st // per_tile) + j] = val[j * per_tile:(j + 1) * per_tile]


def _attn_kernel(q_ref, kc_ref, vc_ref, kp_ref, vp_ref, o_ref, lse_ref):
    i = pl.program_id(2)
    st = ATTN_STEPS
    tq = q_ref.shape[0] * q_ref.shape[1]
    q = q_ref[...].reshape(tq, GROUP_LANES)
    k_rows = jnp.concatenate([kp_ref[...].reshape(st, GROUP_LANES),
                              kc_ref[...].reshape(tq, GROUP_LANES)], axis=0)
    v_rows = jnp.concatenate([vp_ref[...].reshape(st, GROUP_LANES),
                              vc_ref[...].reshape(tq, GROUP_LANES)], axis=0)
    lane = lax.broadcasted_iota(jnp.int32, (1, LANES), 1)
    qi = lax.broadcasted_iota(jnp.int32, (st, st), 0)
    kj = lax.broadcasted_iota(jnp.int32, (st, st), 1)
    prev_visible, cur_visible = kj >= qi, kj <= qi
    has_prev = i > 0
    nt_dims = (((1,), (1,)), ((), ()))
    zero = jnp.zeros((), BF16)

    k_heads, v_heads = [], []
    for pair in range(PAIRS_PER_GROUP):
        sl = slice(pair * LANES, (pair + 1) * LANES)
        kp_, vp_ = k_rows[:, sl], v_rows[:, sl]
        if HEADS_PER_GROUP - 2 * pair >= 2:
            halves = [lane < HEAD_DIM, lane >= HEAD_DIM]
            k_heads.append([jnp.where(m, kp_, zero) for m in halves])
            v_heads.append([jnp.where(m, vp_, zero) for m in halves])
        else:
            k_heads.append([kp_])
            v_heads.append([vp_])

    for a in range(tq // st):
        lse_tile = jnp.zeros((st, LANES), F32)
        o_tiles = []
        for pair in range(PAIRS_PER_GROUP):
            qa = q[a * st:(a + 1) * st, pair * LANES:(pair + 1) * LANES]
            acc = jnp.zeros((st, LANES), F32)
            for hh in range(len(k_heads[pair])):
                kh, vh = k_heads[pair][hh], v_heads[pair][hh]
                s1 = lax.dot_general(qa, kh[a * st:(a + 1) * st], nt_dims,
                                     preferred_element_type=F32)
                s2 = lax.dot_general(qa, kh[(a + 1) * st:(a + 2) * st], nt_dims,
                                     preferred_element_type=F32)
                s1 = jnp.where(prev_visible, s1, NEG_INF)
                if a == 0:
                    s1 = jnp.where(has_prev, s1, NEG_INF)
                s2 = jnp.where(cur_visible, s2, NEG_INF)
                m = jnp.maximum(jnp.max(s1, axis=-1, keepdims=True),
                                jnp.max(s2, axis=-1, keepdims=True))
                p1, p2 = jnp.exp(s1 - m), jnp.exp(s2 - m)
                l = jnp.sum(p1, axis=-1, keepdims=True) + jnp.sum(p2, axis=-1, keepdims=True)
                pv = jnp.dot(p1.astype(BF16), vh[a * st:(a + 1) * st],
                             preferred_element_type=F32)
                pv = pv + jnp.dot(p2.astype(BF16), vh[(a + 1) * st:(a + 2) * st],
                                  preferred_element_type=F32)
                acc = acc + pv * (1.0 / l)
                lse_tile = jnp.where(lane == 2 * pair + hh, m + jnp.log(l), lse_tile)
            o_tiles.append(acc.astype(BF16))
        _store_rows(o_ref, a, jnp.concatenate(o_tiles, axis=1))
        _store_rows(lse_ref, a, lse_tile)


def _dilated_attn(qkv, r, tmp):
    bsz, s, width = qkv.shape
    gl, st = GROUP_LANES, ATTN_STEPS
    cls_rows = tmp // r
    tq = min(512, s // r)
    nt = tq // cls_rows
    n_tiles = s // tmp
    n_q = (s // r) // tq
    sub = tq // st
    main = qkv.reshape(bsz, n_tiles, r, cls_rows, width)

    def main_spec(col):
        return pl.BlockSpec((None, nt, None, cls_rows, gl), lambda b, res, i: (b, i, res, 0, col))

    if cls_rows >= st:
        per_tile = cls_rows // st
        prev_arr = qkv.reshape(bsz, n_tiles, r, per_tile, st, width)

        def prev_spec(col):
            def index(b, res, i):
                c = jnp.maximum(i * sub - 1, 0)
                return (b, c // per_tile, res, c % per_tile, 0, col)
            return pl.BlockSpec((None, None, None, None, st, gl), index)
    else:
        ntp = st // cls_rows
        prev_arr = qkv.reshape(bsz, n_tiles // ntp, ntp, r, cls_rows, width)

        def prev_spec(col):
            return pl.BlockSpec((None, None, ntp, None, cls_rows, gl),
                                lambda b, res, i: (b, jnp.maximum(i * sub - 1, 0), 0, res, 0, col))

    o5 = jax.ShapeDtypeStruct((bsz, n_tiles, r, cls_rows, gl), BF16)
    l5 = jax.ShapeDtypeStruct((bsz, n_tiles, r, cls_rows, LANES), F32)

    o, lse = pl.pallas_call(
        _attn_kernel,
        out_shape=[o5, l5],
        grid=(bsz, r, n_q),
        in_specs=[main_spec(0), main_spec(1), main_spec(2), prev_spec(1), prev_spec(2)],
        out_specs=[
            pl.BlockSpec((None, nt, None, cls_rows, gl), lambda b, res, i: (b, i, res, 0, 0)),
            pl.BlockSpec((None, nt, None, cls_rows, LANES), lambda b, res, i: (b, i, res, 0, 0)),
        ],
        compiler_params=_cparams(("parallel", "parallel", "parallel")),
        name=f"dilated_attn_r{r}",
    )(main, main, main, prev_arr, prev_arr)
    return o.reshape(bsz, s, gl), lse.reshape(bsz, s, LANES)


def _token_major(scr, val, tm, r):
    if r == 1:
        return val
    n = tm // r
    for c in range(scr.shape[0]):
        for res in range(r):
            scr[c, pl.ds(res, n, stride=r), :] = val[res * n:(res + 1) * n, c * LANES:(c + 1) * LANES]
    return jnp.concatenate([scr[c] for c in range(scr.shape[0])], axis=1)


def _attn_out_route_kernel(x_ref, mod_a_ref, mod_m_ref, g_ref, o0_ref, o1_ref, o2_ref,
                           l0_ref, l1_ref, l2_ref, expand_ref, wo_ref, wr_ref,
                           x_out_ref, h_out_ref, route_ref, wts_ref, cnt_ref,
                           o_scr, l_scr, base_scr):
    b, i = pl.program_id(0), pl.program_id(1)
    tm, d = x_ref.shape[1], x_ref.shape[2]
    n_exp = cnt_ref.shape[0]

    @pl.when((b == 0) & (i == 0))
    def _():
        base_scr[...] = jnp.zeros_like(base_scr)

    o_refs, l_refs = (o0_ref, o1_ref, o2_ref), (l0_ref, l1_ref, l2_ref)
    outs, lses = [], []
    for gi, (_, r) in enumerate(ATTN_GROUPS):
        lses.append(_token_major(l_scr.at[gi], l_refs[gi][0], tm, r))
        outs.append(_token_major(o_scr.at[gi], o_refs[gi][0].astype(F32), tm, r))
    m = jnp.maximum(jnp.maximum(lses[0], lses[1]), lses[2])
    es = [jnp.exp(l - m) for l in lses]
    inv = 1.0 / (es[0] + es[1] + es[2])
    scaled = []
    for gi in range(len(ATTN_GROUPS)):
        alpha = es[gi] * inv
        hi = alpha.astype(BF16)
        lo = (alpha - hi.astype(F32)).astype(BF16)
        wide = jnp.dot(hi, expand_ref[...], preferred_element_type=F32) + jnp.dot(
            lo, expand_ref[...], preferred_element_type=F32)
        scaled.append((outs[gi] * wide).astype(BF16))
    attn = jnp.dot(jnp.concatenate(scaled, axis=1), wo_ref[...], preferred_element_type=F32)
    _, _, gate_a = _split_mod(mod_a_ref, b, d)
    x = x_ref[0] + gate_a * attn
    x_out_ref[0] = x

    shift, scale, _ = _split_mod(mod_m_ref, b, d)
    h = _norm_mod(x, g_ref[...], scale, shift)
    for c in range(d // LANES):
        h_out_ref[pl.ds(c, tm, stride=d // LANES), :] = h[:, c * LANES:(c + 1) * LANES]

    lane = lax.broadcasted_iota(jnp.int32, (1, LANES), 1)
    logits = jnp.dot(h, wr_ref[...], precision=lax.Precision.HIGHEST, preferred_element_type=F32)
    logits = jnp.where(lane < n_exp, logits, NEG_INF)
    m1 = jnp.max(logits, axis=-1, keepdims=True)
    i1 = jnp.min(jnp.where(logits == m1, lane, LANES), axis=-1, keepdims=True)
    rest = jnp.where(lane == i1, NEG_INF, logits)
    m2 = jnp.max(rest, axis=-1, keepdims=True)
    i2 = jnp.min(jnp.where(rest == m2, lane, LANES), axis=-1, keepdims=True)
    e2 = jnp.exp(m2 - m1)
    w1 = 1.0 / (1.0 + e2)
    w2 = e2 * w1
    wts_ref[...] = jnp.where(lane == 0, w1, jnp.where(lane == 1, w2, 0.0))

    oh1, oh2 = (lane == i1), (lane == i2)
    picks = jnp.where(oh1 | oh2, 1.0, 0.0).astype(BF16)
    ri = lax.broadcasted_iota(jnp.int32, (tm, tm), 0)
    ci = lax.broadcasted_iota(jnp.int32, (tm, tm), 1)
    before = jnp.where(ci < ri, 1.0, 0.0).astype(BF16)
    rank = jnp.dot(before, picks, preferred_element_type=F32) + base_scr[0:1, :]
    r1 = jnp.sum(jnp.where(oh1, rank, 0.0), axis=-1, keepdims=True).astype(jnp.int32)
    r2 = jnp.sum(jnp.where(oh2, rank, 0.0), axis=-1, keepdims=True).astype(jnp.int32)
    route_ref[...] = jnp.where(lane == 0, r1, jnp.where(lane == 1, r2, jnp.where(
        lane == 2, i1, jnp.where(lane == 3, i2, 0))))
    total = base_scr[0:1, :] + jnp.sum(picks.astype(F32), axis=0, keepdims=True)
    base_scr[...] = jnp.broadcast_to(total, base_scr.shape)
    cnt_ref[...] = jnp.broadcast_to(total, cnt_ref.shape).astype(jnp.int32)


def _attn_out_route(x, mods, set_a, set_m, g, outs, lses, expand, w_o, w_router, tm):
    bsz, s, d = x.shape
    t = bsz * s
    n_i = s // tm
    n_groups = len(ATTN_GROUPS)
    gl = GROUP_LANES
    tok = lambda b, i: (b, i, 0)
    flat = lambda b, i: (b * n_i + i, 0)
    const2 = lambda b, i: (0, 0)
    return pl.pallas_call(
        _attn_out_route_kernel,
        out_shape=[
            jax.ShapeDtypeStruct((bsz, s, d), F32),
            jax.ShapeDtypeStruct((t * ROW_SLAB, LANES), F32),
            jax.ShapeDtypeStruct((t, LANES), jnp.int32),
            jax.ShapeDtypeStruct((t, LANES), F32),
            jax.ShapeDtypeStruct((SUBLANES, LANES), jnp.int32),
        ],
        grid=(bsz, n_i),
        in_specs=[
            pl.BlockSpec((1, tm, d), tok),
            pl.BlockSpec((1, SUBLANES, 3 * d), lambda b, i: (set_a, 0, 0)),
            pl.BlockSpec((1, SUBLANES, 3 * d), lambda b, i: (set_m, 0, 0)),
            pl.BlockSpec((1, d), const2),
        ] + [pl.BlockSpec((1, tm, gl), tok)] * n_groups
          + [pl.BlockSpec((1, tm, LANES), tok)] * n_groups + [
            pl.BlockSpec((LANES, gl), const2),
            pl.BlockSpec((n_groups * gl, d), const2),
            pl.BlockSpec((d, LANES), const2),
        ],
        out_specs=[
            pl.BlockSpec((1, tm, d), tok),
            pl.BlockSpec((tm * ROW_SLAB, LANES), flat),
            pl.BlockSpec((tm, LANES), flat),
            pl.BlockSpec((tm, LANES), flat),
            pl.BlockSpec((SUBLANES, LANES), const2),
        ],
        scratch_shapes=[pltpu.VMEM((n_groups, gl // LANES, tm, LANES), F32),
                        pltpu.VMEM((n_groups, 1, tm, LANES), F32),
                        pltpu.VMEM((SUBLANES, LANES), F32)],
        compiler_params=_cparams(("arbitrary", "arbitrary")),
        name="attn_out_route",
    )(x, mods, mods, g, *outs, *lses, expand, w_o, w_router)


def _row_copy(src, src_row, dst, dst_row, sem):
    return pltpu.make_async_copy(src.at[pl.ds(pl.multiple_of(src_row * ROW_SLAB, ROW_SLAB), ROW_SLAB)],
                                 dst.at[pl.ds(pl.multiple_of(dst_row * ROW_SLAB, ROW_SLAB), ROW_SLAB)],
                                 sem)


def _dispatch_kernel(fill_ref, pos_ref, h_hbm, xs_hbm, zeros_scr, sem, zsem, *, tmd, te, n_exp,
                     n_tiles):
    step = pl.program_id(0)

    def zero_fills():
        def zero_copy(first_row, n):
            start = pl.multiple_of(first_row * ROW_SLAB, ROW_SLAB)
            return pltpu.make_async_copy(zeros_scr.at[pl.ds(0, n * ROW_SLAB)],
                                         xs_hbm.at[pl.ds(start, n * ROW_SLAB)], zsem)
        fills = []
        for e in range(n_exp):
            row, length = fill_ref[e], fill_ref[n_exp + e]
            for bit in reversed(range(te.bit_length() - 1)):
                take = (length & (1 << bit)) != 0
                fills.append((take, zero_copy(row, 1 << bit)))
                row = row + jnp.where(take, 1 << bit, 0)
        n_used = fill_ref[2 * n_exp]
        for k in range(n_exp):
            fills.append((n_used + k < n_tiles, zero_copy((n_used + k) * te, te)))
        return fills

    @pl.when(step == 0)
    def _():
        zeros_scr[...] = jnp.zeros_like(zeros_scr)
        for take, cp in zero_fills():
            pl.when(take)(cp.start)
        for take, cp in zero_fills():
            pl.when(take)(cp.wait)

    def copy(j):
        tok = step * tmd + j // TOP_K
        return _row_copy(h_hbm, tok, xs_hbm, pos_ref[0, 0, j], sem)

    def issue(j, carry):
        copy(j).start()
        return carry

    def drain(j, carry):
        copy(j).wait()
        return carry

    lax.fori_loop(0, tmd * TOP_K, issue, 0)
    lax.fori_loop(0, tmd * TOP_K, drain, 0)


def _moe_dispatch(fill, pos_blocks, h_rows, n_tiles, n_exp, tmd, te):
    n_steps = pos_blocks.shape[0]
    assert te & (te - 1) == 0
    kernel = functools.partial(_dispatch_kernel, tmd=tmd, te=te, n_exp=n_exp, n_tiles=n_tiles)
    return pl.pallas_call(
        kernel,
        out_shape=jax.ShapeDtypeStruct((n_tiles * te * ROW_SLAB, LANES), F32),
        grid_spec=pltpu.PrefetchScalarGridSpec(
            num_scalar_prefetch=1,
            grid=(n_steps,),
            in_specs=[
                pl.BlockSpec((1, 1, tmd * TOP_K), lambda s, cnt: (s, 0, 0), memory_space=pltpu.SMEM),
                pl.BlockSpec(memory_space=pl.ANY),
            ],
            out_specs=pl.BlockSpec(memory_space=pl.ANY),
            scratch_shapes=[pltpu.VMEM((te * ROW_SLAB, LANES), F32), pltpu.SemaphoreType.DMA,
                            pltpu.SemaphoreType.DMA],
        ),
        compiler_params=_cparams(("arbitrary",)),
        name="moe_dispatch",
    )(fill, pos_blocks, h_rows)


def _slab_rows_to_matrix(ref, n_rows, n_chunks):
    return jnp.concatenate([ref[pl.ds(c, n_rows, stride=n_chunks), :] for c in range(n_chunks)],
                           axis=1)


def _experts_kernel(tile_expert_ref, n_used_ref, x_ref, wg_ref, wu_ref, wd_ref, y_ref, *, te,
                    f_chunks):
    del tile_expert_ref
    used = pl.program_id(0) < n_used_ref[0]

    @pl.when(jnp.logical_not(used))
    def _():
        y_ref[...] = jnp.zeros_like(y_ref)

    @pl.when(used)
    def _():
        d = wg_ref.shape[1]
        n_chunks = d // LANES
        x = _slab_rows_to_matrix(x_ref, te, n_chunks).astype(BF16)
        y = jnp.zeros((te, d), F32)
        for lo, hi in f_chunks:
            gate = jnp.dot(x, wg_ref[0, :, lo:hi], preferred_element_type=F32)
            up = jnp.dot(x, wu_ref[0, :, lo:hi], preferred_element_type=F32)
            y = y + jnp.dot((_silu(gate) * up).astype(BF16), wd_ref[0, lo:hi, :],
                            preferred_element_type=F32)
        for c in range(n_chunks):
            y_ref[pl.ds(c, te, stride=n_chunks), :] = y[:, c * LANES:(c + 1) * LANES]


def _f_chunks(f):
    if f <= 1536:
        return ((0, f),)
    half = (f // 2 + 255) // 256 * 256
    return ((0, half), (half, f))


def _moe_experts(tile_expert, n_used, xs, w_gate, w_up, w_down, te):
    n_tiles = tile_expert.shape[0]
    _, d, f = w_gate.shape
    kernel = functools.partial(_experts_kernel, te=te, f_chunks=_f_chunks(f))
    weights = lambda i, te_, nu_: (te_[i], 0, 0)
    return pl.pallas_call(
        kernel,
        out_shape=jax.ShapeDtypeStruct(xs.shape, F32),
        grid_spec=pltpu.PrefetchScalarGridSpec(
            num_scalar_prefetch=2,
            grid=(n_tiles,),
            in_specs=[
                pl.BlockSpec((te * ROW_SLAB, LANES), lambda i, te_, nu_: (jnp.minimum(i, nu_[0] - 1), 0)),
                pl.BlockSpec((1, d, f), weights),
                pl.BlockSpec((1, d, f), weights),
                pl.BlockSpec((1, f, d), weights),
            ],
            out_specs=pl.BlockSpec((te * ROW_SLAB, LANES), lambda i, te_, nu_: (i, 0)),
        ),
        compiler_params=_cparams(("arbitrary",)),
        name="moe_experts",
    )(tile_expert, n_used, xs, w_gate, w_up, w_down)


def _combine_kernel(pos_ref, x_ref, mod_ref, g_ref, wts_ref, ys_hbm, o_ref, buf, sem, *, tmc):
    b = pl.program_id(0)
    d = x_ref.shape[2]
    n_chunks = d // LANES

    def copy(j):
        k = j % TOP_K
        dst_row = k * tmc + j // TOP_K
        return _row_copy(ys_hbm, pos_ref[0, 0, j], buf, dst_row, sem)

    def issue(j, carry):
        copy(j).start()
        return carry

    def drain(j, carry):
        copy(j).wait()
        return carry

    lax.fori_loop(0, tmc * TOP_K, issue, 0)
    lax.fori_loop(0, tmc * TOP_K, drain, 0)

    wts = wts_ref[...]
    y = jnp.zeros((tmc, d), F32)
    for k in range(TOP_K):
        yk = _slab_rows_to_matrix(buf.at[pl.ds(k * tmc * ROW_SLAB, tmc * ROW_SLAB)], tmc, n_chunks)
        y = y + wts[:, k:k + 1] * yk
    _, _, gate = _split_mod(mod_ref, b, d)
    x = x_ref[0] + gate * y
    o_ref[0] = x * lax.rsqrt(jnp.mean(x * x, axis=-1, keepdims=True) + RMS_EPS) * g_ref[...]


def _moe_combine(pos_blocks, x, mods, set_idx, final_g, wts, ys, tmc):
    bsz, s, d = x.shape
    n_i = s // tmc
    kernel = functools.partial(_combine_kernel, tmc=tmc)
    return pl.pallas_call(
        kernel,
        out_shape=jax.ShapeDtypeStruct(x.shape, F32),
        grid=(bsz, n_i),
        in_specs=[
            pl.BlockSpec((1, 1, tmc * TOP_K), lambda b, i: (b * n_i + i, 0, 0),
                         memory_space=pltpu.SMEM),
            pl.BlockSpec((1, tmc, d), lambda b, i: (b, i, 0)),
            pl.BlockSpec((1, SUBLANES, 3 * d), lambda b, i: (set_idx, 0, 0)),
            pl.BlockSpec((1, d), lambda b, i: (0, 0)),
            pl.BlockSpec((tmc, LANES), lambda b, i: (b * n_i + i, 0)),
            pl.BlockSpec(memory_space=pl.ANY),
        ],
        out_specs=pl.BlockSpec((1, tmc, d), lambda b, i: (b, i, 0)),
        scratch_shapes=[pltpu.VMEM((TOP_K * tmc * ROW_SLAB, LANES), F32), pltpu.SemaphoreType.DMA],
        compiler_params=_cparams(("arbitrary", "arbitrary")),
        name="moe_combine",
    )(pos_blocks, x, mods, final_g, wts, ys)


def _pad_group_columns(w, which):
    width = HEADS_PER_GROUP * HEAD_DIM
    n_groups = len(ATTN_GROUPS)
    cols = w[:, which * n_groups * width:(which + 1) * n_groups * width]
    cols = cols.reshape(w.shape[0], n_groups, width)
    return jnp.pad(cols, ((0, 0), (0, 0), (0, GROUP_LANES - width)))


def kernel(x, c, positions, mod_w, mod_b, norm_g, conv_w_in, conv_w, conv_w_out, ffn_w_gate,
           ffn_w_up, ffn_w_down, attn_w_qkv, attn_w_o, router_w, moe_w_gate, moe_w_up, moe_w_down,
           final_g):
    bsz, s, d = x.shape
    t = bsz * s
    n_groups = len(ATTN_GROUPS)
    n_exp = router_w.shape[-1]
    assert all(w // r == ATTN_STEPS for w, r in ATTN_GROUPS)
    assert bsz <= SUBLANES and n_exp <= SUBLANES and d == ROW_SLAB * LANES
    tm = min(512, s)
    assert s % tm == 0 and all(s % (r * ATTN_STEPS) == 0 and tm % r == 0 for _, r in ATTN_GROUPS)

    c_pad = jnp.pad(c.astype(F32), ((0, SUBLANES - bsz), (0, 0)))
    mods = _ada_params(c_pad, mod_w.reshape(-1, d, 3 * d), mod_b.reshape(-1, 1, 3 * d))

    x = _conv_mixer(x, mods, 0, norm_g[0, 0][None], conv_w_in[0].astype(BF16), conv_w[0],
                    conv_w_out[0].astype(BF16), tm)
    x = _dense_ffn(x, mods, 1, norm_g[0, 1][None], ffn_w_gate[0].astype(BF16),
                   ffn_w_up[0].astype(BF16), ffn_w_down[0].astype(BF16), tm)

    w_qkv = attn_w_qkv[0]
    w_groups = jnp.concatenate([_pad_group_columns(w_qkv, which) for which in range(3)], axis=2)
    w_groups = w_groups.transpose(1, 0, 2).astype(BF16)
    inv_freq = ROPE_THETA ** (-jnp.arange(0, HEAD_DIM, 2, dtype=F32) / HEAD_DIM)
    freq = jnp.tile(inv_freq, LANES // (HEAD_DIM // 2))[None]
    qkvs = _qkv_rope(x, mods, 2, norm_g[1, 0][None], positions[..., None], freq, w_groups, tm)
    outs, lses = zip(*[_dilated_attn(qkv, r, tm) for qkv, (_, r) in zip(qkvs, ATTN_GROUPS)])

    width = HEADS_PER_GROUP * HEAD_DIM
    w_o = jnp.pad(attn_w_o[0].reshape(n_groups, width, d), ((0, 0), (0, GROUP_LANES - width), (0, 0)))
    w_o = w_o.reshape(n_groups * GROUP_LANES, d).astype(BF16)
    expand = (jnp.arange(GROUP_LANES)[None, :] // HEAD_DIM == jnp.arange(LANES)[:, None]).astype(BF16)
    w_router = jnp.pad(router_w[0], ((0, 0), (0, LANES - n_exp)))

    x, h_rows, route, wts, counts = _attn_out_route(
        x, mods, 2, 3, norm_g[1, 1][None], outs, lses, expand, w_o, w_router, tm)

    te = min(512, t)
    n_tiles = (t * TOP_K) // te + n_exp
    counts = counts[0, :n_exp]
    tiles_per_expert = (counts + te - 1) // te
    ends = jnp.cumsum(tiles_per_expert)
    first_row = (ends - tiles_per_expert) * te
    n_used = ends[-1:]
    pos_flat = (first_row[route[:, TOP_K:2 * TOP_K]] + route[:, :TOP_K]).reshape(-1)
    fill = jnp.concatenate([first_row + counts, tiles_per_expert * te - counts, n_used]).astype(jnp.int32)
    idx = jnp.minimum(jnp.arange(n_tiles, dtype=jnp.int32), n_used - 1)
    tile_expert = jnp.sum(idx[:, None] >= ends[None, :], axis=1).astype(jnp.int32)

    tmd = min(1024, t)
    xs = _moe_dispatch(fill, pos_flat.reshape(t // tmd, 1, tmd * TOP_K), h_rows, n_tiles, n_exp, tmd, te)
    ys = _moe_experts(tile_expert, n_used.astype(jnp.int32), xs, moe_w_gate[0].astype(BF16),
                      moe_w_up[0].astype(BF16), moe_w_down[0].astype(BF16), te)

    tmc = min(256, s)
    return _moe_combine(pos_flat.reshape(t // tmc, 1, tmc * TOP_K), x, mods, 3, final_g[None], wts,
                        ys, tmc)
```

```python
import functools
import math

import jax
import jax.numpy as jnp
from jax import lax
from jax.experimental import pallas as pl
from jax.experimental.pallas import tpu as pltpu

CONV_WIDTH = 3
ATTN_GROUPS = ((128, 1), (512, 4), (2048, 16))
HEADS_PER_GROUP = 5
HEAD_DIM = 64
ROPE_THETA = 10000.0
TOP_K = 2
RMS_EPS = 1e-6

LANES = 128
SUBLANES = 8
VMEM_LIMIT_BYTES = 56 * 1024 * 1024

GROUP_LANES = 384
PAIRS_PER_GROUP = GROUP_LANES // LANES
ATTN_STEPS = 128
ROW_SLAB = 8

F32 = jnp.float32
BF16 = jnp.bfloat16
NEG_INF = float("-inf")


def _cparams(semantics):
    return pltpu.CompilerParams(dimension_semantics=semantics, vmem_limit_bytes=VMEM_LIMIT_BYTES)


def _norm_mod(x, g, scale, shift):
    y = x * lax.rsqrt(jnp.mean(x * x, axis=-1, keepdims=True) + RMS_EPS)
    return (y * g) * (1.0 + scale) + shift


def _split_mod(mod_ref, b, d):
    row = mod_ref[0, pl.ds(b, 1), :]
    return row[:, :d], row[:, d:2 * d], row[:, 2 * d:]


def _silu(x):
    return x * jax.nn.sigmoid(x)


def _ada_kernel(c_ref, w_ref, b_ref, o_ref):
    s = _silu(c_ref[...])
    o_ref[0] = jnp.dot(s, w_ref[0], precision=lax.Precision.HIGHEST,
                       preferred_element_type=F32) + b_ref[0]


def _ada_params(c_pad, mod_w, mod_b):
    n_sets, d, d3 = mod_w.shape
    tn = d
    return pl.pallas_call(
        _ada_kernel,
        out_shape=jax.ShapeDtypeStruct((n_sets, SUBLANES, d3), F32),
        grid=(n_sets, d3 // tn),
        in_specs=[
            pl.BlockSpec((SUBLANES, d), lambda s, j: (0, 0)),
            pl.BlockSpec((1, d, tn), lambda s, j: (s, 0, j)),
            pl.BlockSpec((1, 1, tn), lambda s, j: (s, 0, j)),
        ],
        out_specs=pl.BlockSpec((1, SUBLANES, tn), lambda s, j: (s, 0, j)),
        compiler_params=_cparams(("parallel", "parallel")),
        name="ada_params",
    )(c_pad, mod_w, mod_b)


def _conv_mixer_kernel(x_ref, mod_ref, g_ref, win_ref, cw_ref, wout_ref, o_ref, carry_ref):
    b, i = pl.program_id(0), pl.program_id(1)
    tm, d = x_ref.shape[1], x_ref.shape[2]

    @pl.when(i == 0)
    def _():
        carry_ref[...] = jnp.zeros_like(carry_ref)

    x = x_ref[0]
    shift, scale, gate = _split_mod(mod_ref, b, d)
    h = _norm_mod(x, g_ref[...], scale, shift).astype(BF16)
    bcu = jnp.dot(h, win_ref[...], preferred_element_type=F32)
    b_gate, v = bcu[:, :d], bcu[:, d:2 * d] * bcu[:, 2 * d:]
    row = lax.broadcasted_iota(jnp.int32, (tm, 1), 0)
    prev = carry_ref[...]
    v1 = jnp.where(row == 0, prev[7:8], pltpu.roll(v, 1, 0))
    v2 = jnp.where(row == 0, prev[6:7], jnp.where(row == 1, prev[7:8], pltpu.roll(v, 2, 0)))
    cw = cw_ref[...]
    conv = cw[0:1] * v2 + cw[1:2] * v1 + cw[2:3] * v
    carry_ref[...] = v[tm - SUBLANES:, :]
    y = jnp.dot((b_gate * conv).astype(BF16), wout_ref[...], preferred_element_type=F32)
    o_ref[0] = x + gate * y


def _conv_mixer(x, mods, set_idx, g, w_in, conv_w, w_out, tm):
    bsz, s, d = x.shape
    return pl.pallas_call(
        _conv_mixer_kernel,
        out_shape=jax.ShapeDtypeStruct(x.shape, F32),
        grid=(bsz, s // tm),
        in_specs=[
            pl.BlockSpec((1, tm, d), lambda b, i: (b, i, 0)),
            pl.BlockSpec((1, SUBLANES, 3 * d), lambda b, i: (set_idx, 0, 0)),
            pl.BlockSpec((1, d), lambda b, i: (0, 0)),
            pl.BlockSpec((d, 3 * d), lambda b, i: (0, 0)),
            pl.BlockSpec((CONV_WIDTH, d), lambda b, i: (0, 0)),
            pl.BlockSpec((d, d), lambda b, i: (0, 0)),
        ],
        out_specs=pl.BlockSpec((1, tm, d), lambda b, i: (b, i, 0)),
        scratch_shapes=[pltpu.VMEM((SUBLANES, d), F32)],
        compiler_params=_cparams(("arbitrary", "arbitrary")),
        name="conv_mixer",
    )(x, mods, g, w_in, conv_w, w_out)


def _ffn_kernel(x_ref, mod_ref, g_ref, wg_ref, wu_ref, wd_ref, o_ref):
    b = pl.program_id(0)
    d = x_ref.shape[2]
    x = x_ref[0]
    shift, scale, gate = _split_mod(mod_ref, b, d)
    h = _norm_mod(x, g_ref[...], scale, shift).astype(BF16)
    a = _silu(jnp.dot(h, wg_ref[...], preferred_element_type=F32)) * jnp.dot(
        h, wu_ref[...], preferred_element_type=F32)
    y = jnp.dot(a.astype(BF16), wd_ref[...], preferred_element_type=F32)
    o_ref[0] = x + gate * y


def _dense_ffn(x, mods, set_idx, g, w_gate, w_up, w_down, tm):
    bsz, s, d = x.shape
    f = w_gate.shape[1]
    resident = pl.Buffered(1)
    return pl.pallas_call(
        _ffn_kernel,
        out_shape=jax.ShapeDtypeStruct(x.shape, F32),
        grid=(bsz, s // tm),
        in_specs=[
            pl.BlockSpec((1, tm, d), lambda b, i: (b, i, 0)),
            pl.BlockSpec((1, SUBLANES, 3 * d), lambda b, i: (set_idx, 0, 0)),
            pl.BlockSpec((1, d), lambda b, i: (0, 0)),
            pl.BlockSpec((d, f), lambda b, i: (0, 0), pipeline_mode=resident),
            pl.BlockSpec((d, f), lambda b, i: (0, 0), pipeline_mode=resident),
            pl.BlockSpec((f, d), lambda b, i: (0, 0), pipeline_mode=resident),
        ],
        out_specs=pl.BlockSpec((1, tm, d), lambda b, i: (b, i, 0)),
        compiler_params=_cparams(("parallel", "parallel")),
        name="dense_ffn",
    )(x, mods, g, w_gate, w_up, w_down)


def _class_major(ref, tm, r):
    n_chunks = ref.shape[0]
    if r == 1:
        return jnp.concatenate([ref[c] for c in range(n_chunks)], axis=1)
    return jnp.concatenate(
        [jnp.concatenate([ref[c, pl.ds(res, tm // r, stride=r), :] for c in range(n_chunks)], axis=1)
         for res in range(r)], axis=0)


def _store_lane_chunks(ref, val):
    for c in range(ref.shape[0]):
        ref[c] = val[:, c * LANES:(c + 1) * LANES]


def _rope(t, cos, sin_signed, first_half):
    out = []
    for c in range(t.shape[1] // LANES):
        tc = t[:, c * LANES:(c + 1) * LANES]
        rot = jnp.where(first_half, pltpu.roll(tc, LANES - HEAD_DIM // 2, 1),
                        pltpu.roll(tc, HEAD_DIM // 2, 1))
        out.append(tc * cos + rot * sin_signed)
    return jnp.concatenate(out, axis=1)


def _qkv_kernel(x_ref, mod_ref, g_ref, pos_ref, freq_ref, w_ref, o0_ref, o1_ref, o2_ref,
                h_scr, cos_scr, sin_scr):
    b = pl.program_id(0)
    tm, d = x_ref.shape[1], x_ref.shape[2]
    shift, scale, _ = _split_mod(mod_ref, b, d)
    _store_lane_chunks(h_scr, _norm_mod(x_ref[0], g_ref[...], scale, shift))
    ang = pos_ref[0].astype(F32) * freq_ref[...]
    first_half = (lax.broadcasted_iota(jnp.int32, (1, LANES), 1) % HEAD_DIM) < HEAD_DIM // 2
    cos_scr[0] = jnp.cos(ang)
    sin_scr[0] = jnp.where(first_half, -jnp.sin(ang), jnp.sin(ang))
    gl = GROUP_LANES
    for gi, ((_, r), o_ref) in enumerate(zip(ATTN_GROUPS, (o0_ref, o1_ref, o2_ref))):
        hg = _class_major(h_scr, tm, r).astype(BF16)
        cos, sin_signed = _class_major(cos_scr, tm, r), _class_major(sin_scr, tm, r)
        qkv = jnp.dot(hg, w_ref[gi], preferred_element_type=F32)
        q = _rope(qkv[:, :gl], cos, sin_signed, first_half) * (HEAD_DIM ** -0.5)
        k = _rope(qkv[:, gl:2 * gl], cos, sin_signed, first_half)
        o_ref[0, :, :gl] = q.astype(BF16)
        o_ref[0, :, gl:2 * gl] = k.astype(BF16)
        o_ref[0, :, 2 * gl:] = qkv[:, 2 * gl:].astype(BF16)


def _qkv_rope(x, mods, set_idx, g, pos3, freq, w_groups, tm):
    bsz, s, d = x.shape
    n_groups = len(ATTN_GROUPS)
    out = jax.ShapeDtypeStruct((bsz, s, 3 * GROUP_LANES), BF16)
    return pl.pallas_call(
        _qkv_kernel,
        out_shape=[out] * n_groups,
        grid=(bsz, s // tm),
        in_specs=[
            pl.BlockSpec((1, tm, d), lambda b, i: (b, i, 0)),
            pl.BlockSpec((1, SUBLANES, 3 * d), lambda b, i: (set_idx, 0, 0)),
            pl.BlockSpec((1, d), lambda b, i: (0, 0)),
            pl.BlockSpec((1, tm, 1), lambda b, i: (b, i, 0)),
            pl.BlockSpec((1, LANES), lambda b, i: (0, 0)),
            pl.BlockSpec((n_groups, d, 3 * GROUP_LANES), lambda b, i: (0, 0, 0)),
        ],
        out_specs=[pl.BlockSpec((1, tm, 3 * GROUP_LANES), lambda b, i: (b, i, 0))] * n_groups,
        scratch_shapes=[pltpu.VMEM((d // LANES, tm, LANES), F32), pltpu.VMEM((1, tm, LANES), F32),
                        pltpu.VMEM((1, tm, LANES), F32)],
        compiler_params=_cparams(("parallel", "parallel")),
        name="qkv_rope",
    )(x, mods, g, pos3, freq, w_groups)


def _store_rows(ref, a, val):
    st, per_tile = ATTN_STEPS, ref.shape[1]
    if per_tile >= st:
        ref[(a * st) // per_tile, pl.ds((a * st) % per_tile, st), :] = val
    else:
        for j in range(st // per_tile):
            ref[a * (st // per_tile) + j] = val[j * per_tile:(j + 1) * per_tile]


def _attn_kernel(q_ref, kc_ref, vc_ref, kp_ref, vp_ref, o_ref, lse_ref):
    i = pl.program_id(2)
    st = ATTN_STEPS
    tq = q_ref.shape[0] * q_ref.shape[1]
    q = q_ref[...].reshape(tq, GROUP_LANES)
    k_rows = jnp.concatenate([kp_ref[...].reshape(st, GROUP_LANES),
                              kc_ref[...].reshape(tq, GROUP_LANES)], axis=0)
    v_rows = jnp.concatenate([vp_ref[...].reshape(st, GROUP_LANES),
                              vc_ref[...].reshape(tq, GROUP_LANES)], axis=0)
    lane = lax.broadcasted_iota(jnp.int32, (1, LANES), 1)
    qi = lax.broadcasted_iota(jnp.int32, (st, st), 0)
    kj = lax.broadcasted_iota(jnp.int32, (st, st), 1)
    prev_visible, cur_visible = kj >= qi, kj <= qi
    has_prev = i > 0
    nt_dims = (((1,), (1,)), ((), ()))
    zero = jnp.zeros((), BF16)

    k_heads, v_heads = [], []
    for pair in range(PAIRS_PER_GROUP):
        sl = slice(pair * LANES, (pair + 1) * LANES)
        kp_, vp_ = k_rows[:, sl], v_rows[:, sl]
        if HEADS_PER_GROUP - 2 * pair >= 2:
            halves = [lane < HEAD_DIM, lane >= HEAD_DIM]
            k_heads.append([jnp.where(m, kp_, zero) for m in halves])
            v_heads.append([jnp.where(m, vp_, zero) for m in halves])
        else:
            k_heads.append([kp_])
            v_heads.append([vp_])

    for a in range(tq // st):
        lse_tile = jnp.zeros((st, LANES), F32)
        o_tiles = []
        for pair in range(PAIRS_PER_GROUP):
            qa = q[a * st:(a + 1) * st, pair * LANES:(pair + 1) * LANES]
            acc = jnp.zeros((st, LANES), F32)
            for hh in range(len(k_heads[pair])):
                kh, vh = k_heads[pair][hh], v_heads[pair][hh]
                s1 = lax.dot_general(qa, kh[a * st:(a + 1) * st], nt_dims,
                                     preferred_element_type=F32)
                s2 = lax.dot_general(qa, kh[(a + 1) * st:(a + 2) * st], nt_dims,
                                     preferred_element_type=F32)
                s1 = jnp.where(prev_visible, s1, NEG_INF)
                if a == 0:
                    s1 = jnp.where(has_prev, s1, NEG_INF)
                s2 = jnp.where(cur_visible, s2, NEG_INF)
                m = jnp.maximum(jnp.max(s1, axis=-1, keepdims=True),
                                jnp.max(s2, axis=-1, keepdims=True))
                p1, p2 = jnp.exp(s1 - m), jnp.exp(s2 - m)
                l = jnp.sum(p1, axis=-1, keepdims=True) + jnp.sum(p2, axis=-1, keepdims=True)
                pv = jnp.dot(p1.astype(BF16), vh[a * st:(a + 1) * st],
                             preferred_element_type=F32)
                pv = pv + jnp.dot(p2.astype(BF16), vh[(a + 1) * st:(a + 2) * st],
                                  preferred_element_type=F32)
                acc = acc + pv * (1.0 / l)
                lse_tile = jnp.where(lane == 2 * pair + hh, m + jnp.log(l), lse_tile)
            o_tiles.append(acc.astype(BF16))
        _store_rows(o_ref, a, jnp.concatenate(o_tiles, axis=1))
        _store_rows(lse_ref, a, lse_tile)


def _dilated_attn(qkv, r, tmp):
    bsz, s, width = qkv.shape
    gl, st = GROUP_LANES, ATTN_STEPS
    cls_rows = tmp // r
    tq = min(512, s // r)
    nt = tq // cls_rows
    n_tiles = s // tmp
    n_q = (s // r) // tq
    sub = tq // st
    main = qkv.reshape(bsz, n_tiles, r, cls_rows, width)

    def main_spec(col):
        return pl.BlockSpec((None, nt, None, cls_rows, gl), lambda b, res, i: (b, i, res, 0, col))

    if cls_rows >= st:
        per_tile = cls_rows // st
        prev_arr = qkv.reshape(bsz, n_tiles, r, per_tile, st, width)

        def prev_spec(col):
            def index(b, res, i):
                c = jnp.maximum(i * sub - 1, 0)
                return (b, c // per_tile, res, c % per_tile, 0, col)
            return pl.BlockSpec((None, None, None, None, st, gl), index)
    else:
        ntp = st // cls_rows
        prev_arr = qkv.reshape(bsz, n_tiles // ntp, ntp, r, cls_rows, width)

        def prev_spec(col):
            return pl.BlockSpec((None, None, ntp, None, cls_rows, gl),
                                lambda b, res, i: (b, jnp.maximum(i * sub - 1, 0), 0, res, 0, col))

    o5 = jax.ShapeDtypeStruct((bsz, n_tiles, r, cls_rows, gl), BF16)
    l5 = jax.ShapeDtypeStruct((bsz, n_tiles, r, cls_rows, LANES), F32)

    o, lse = pl.pallas_call(
        _attn_kernel,
        out_shape=[o5, l5],
        grid=(bsz, r, n_q),
        in_specs=[main_spec(0), main_spec(1), main_spec(2), prev_spec(1), prev_spec(2)],
        out_specs=[
            pl.BlockSpec((None, nt, None, cls_rows, gl), lambda b, res, i: (b, i, res, 0, 0)),
            pl.BlockSpec((None, nt, None, cls_rows, LANES), lambda b, res, i: (b, i, res, 0, 0)),
        ],
        compiler_params=_cparams(("parallel", "parallel", "parallel")),
        name=f"dilated_attn_r{r}",
    )(main, main, main, prev_arr, prev_arr)
    return o.reshape(bsz, s, gl), lse.reshape(bsz, s, LANES)


def _token_major(scr, val, tm, r):
    if r == 1:
        return val
    n = tm // r
    for c in range(scr.shape[0]):
        for res in range(r):
            scr[c, pl.ds(res, n, stride=r), :] = val[res * n:(res + 1) * n, c * LANES:(c + 1) * LANES]
    return jnp.concatenate([scr[c] for c in range(scr.shape[0])], axis=1)


def _attn_out_route_kernel(x_ref, mod_a_ref, mod_m_ref, g_ref, o0_ref, o1_ref, o2_ref,
                           l0_ref, l1_ref, l2_ref, expand_ref, wo_ref, wr_ref,
                           x_out_ref, h_out_ref, route_ref, wts_ref, cnt_ref,
                           o_scr, l_scr, base_scr):
    b, i = pl.program_id(0), pl.program_id(1)
    tm, d = x_ref.shape[1], x_ref.shape[2]
    n_exp = cnt_ref.shape[0]

    @pl.when((b == 0) & (i == 0))
    def _():
        base_scr[...] = jnp.zeros_like(base_scr)

    o_refs, l_refs = (o0_ref, o1_ref, o2_ref), (l0_ref, l1_ref, l2_ref)
    outs, lses = [], []
    for gi, (_, r) in enumerate(ATTN_GROUPS):
        lses.append(_token_major(l_scr.at[gi], l_refs[gi][0], tm, r))
        outs.append(_token_major(o_scr.at[gi], o_refs[gi][0].astype(F32), tm, r))
    m = jnp.maximum(jnp.maximum(lses[0], lses[1]), lses[2])
    es = [jnp.exp(l - m) for l in lses]
    inv = 1.0 / (es[0] + es[1] + es[2])
    scaled = []
    for gi in range(len(ATTN_GROUPS)):
        alpha = es[gi] * inv
        hi = alpha.astype(BF16)
        lo = (alpha - hi.astype(F32)).astype(BF16)
        wide = jnp.dot(hi, expand_ref[...], preferred_element_type=F32) + jnp.dot(
            lo, expand_ref[...], preferred_element_type=F32)
        scaled.append((outs[gi] * wide).astype(BF16))
    attn = jnp.dot(jnp.concatenate(scaled, axis=1), wo_ref[...], preferred_element_type=F32)
    _, _, gate_a = _split_mod(mod_a_ref, b, d)
    x = x_ref[0] + gate_a * attn
    x_out_ref[0] = x

    shift, scale, _ = _split_mod(mod_m_ref, b, d)
    h = _norm_mod(x, g_ref[...], scale, shift)
    for c in range(d // LANES):
        h_out_ref[pl.ds(c, tm, stride=d // LANES), :] = h[:, c * LANES:(c + 1) * LANES]

    lane = lax.broadcasted_iota(jnp.int32, (1, LANES), 1)
    logits = jnp.dot(h, wr_ref[...], precision=lax.Precision.HIGHEST, preferred_element_type=F32)
    logits = jnp.where(lane < n_exp, logits, NEG_INF)
    m1 = jnp.max(logits, axis=-1, keepdims=True)
    i1 = jnp.min(jnp.where(logits == m1, lane, LANES), axis=-1, keepdims=True)
    rest = jnp.where(lane == i1, NEG_INF, logits)
    m2 = jnp.max(rest, axis=-1, keepdims=True)
    i2 = jnp.min(jnp.where(rest == m2, lane, LANES), axis=-1, keepdims=True)
    e2 = jnp.exp(m2 - m1)
    w1 = 1.0 / (1.0 + e2)
    w2 = e2 * w1
    wts_ref[...] = jnp.where(lane == 0, w1, jnp.where(lane == 1, w2, 0.0))

    oh1, oh2 = (lane == i1), (lane == i2)
    picks = jnp.where(oh1 | oh2, 1.0, 0.0).astype(BF16)
    ri = lax.broadcasted_iota(jnp.int32, (tm, tm), 0)
    ci = lax.broadcasted_iota(jnp.int32, (tm, tm), 1)
    before = jnp.where(ci < ri, 1.0, 0.0).astype(BF16)
    rank = jnp.dot(before, picks, preferred_element_type=F32) + base_scr[0:1, :]
    r1 = jnp.sum(jnp.where(oh1, rank, 0.0), axis=-1, keepdims=True).astype(jnp.int32)
    r2 = jnp.sum(jnp.where(oh2, rank, 0.0), axis=-1, keepdims=True).astype(jnp.int32)
    route_ref[...] = jnp.where(lane == 0, r1, jnp.where(lane == 1, r2, jnp.where(
        lane == 2, i1, jnp.where(lane == 3, i2, 0))))
    total = base_scr[0:1, :] + jnp.sum(picks.astype(F32), axis=0, keepdims=True)
    base_scr[...] = jnp.broadcast_to(total, base_scr.shape)
    cnt_ref[...] = jnp.broadcast_to(total, cnt_ref.shape).astype(jnp.int32)


def _attn_out_route(x, mods, set_a, set_m, g, outs, lses, expand, w_o, w_router, tm):
    bsz, s, d = x.shape
    t = bsz * s
    n_i = s // tm
    n_groups = len(ATTN_GROUPS)
    gl = GROUP_LANES
    tok = lambda b, i: (b, i, 0)
    flat = lambda b, i: (b * n_i + i, 0)
    const2 = lambda b, i: (0, 0)
    return pl.pallas_call(
        _attn_out_route_kernel,
        out_shape=[
            jax.ShapeDtypeStruct((bsz, s, d), F32),
            jax.ShapeDtypeStruct((t * ROW_SLAB, LANES), F32),
            jax.ShapeDtypeStruct((t, LANES), jnp.int32),
            jax.ShapeDtypeStruct((t, LANES), F32),
            jax.ShapeDtypeStruct((SUBLANES, LANES), jnp.int32),
        ],
        grid=(bsz, n_i),
        in_specs=[
            pl.BlockSpec((1, tm, d), tok),
            pl.BlockSpec((1, SUBLANES, 3 * d), lambda b, i: (set_a, 0, 0)),
            pl.BlockSpec((1, SUBLANES, 3 * d), lambda b, i: (set_m, 0, 0)),
            pl.BlockSpec((1, d), const2),
        ] + [pl.BlockSpec((1, tm, gl), tok)] * n_groups
          + [pl.BlockSpec((1, tm, LANES), tok)] * n_groups + [
            pl.BlockSpec((LANES, gl), const2),
            pl.BlockSpec((n_groups * gl, d), const2),
            pl.BlockSpec((d, LANES), const2),
        ],
        out_specs=[
            pl.BlockSpec((1, tm, d), tok),
            pl.BlockSpec((tm * ROW_SLAB, LANES), flat),
            pl.BlockSpec((tm, LANES), flat),
            pl.BlockSpec((tm, LANES), flat),
            pl.BlockSpec((SUBLANES, LANES), const2),
        ],
        scratch_shapes=[pltpu.VMEM((n_groups, gl // LANES, tm, LANES), F32),
                        pltpu.VMEM((n_groups, 1, tm, LANES), F32),
                        pltpu.VMEM((SUBLANES, LANES), F32)],
        compiler_params=_cparams(("arbitrary", "arbitrary")),
        name="attn_out_route",
    )(x, mods, mods, g, *outs, *lses, expand, w_o, w_router)


DMA_LOOP_UNROLL = 16


def _for_each(n, fn):
    def body(g, carry):
        for u in range(DMA_LOOP_UNROLL):
            fn(g * DMA_LOOP_UNROLL + u)
        return carry
    lax.fori_loop(0, n // DMA_LOOP_UNROLL, body, 0)


def _row_copy(src, src_row, dst, dst_row, sem):
    return pltpu.make_async_copy(src.at[pl.ds(pl.multiple_of(src_row * ROW_SLAB, ROW_SLAB), ROW_SLAB)],
                                 dst.at[pl.ds(pl.multiple_of(dst_row * ROW_SLAB, ROW_SLAB), ROW_SLAB)],
                                 sem)


def _dispatch_kernel(fill_ref, pos_ref, h_ref, xs_hbm, zeros_scr, sem, zsem, *, tmd, te, n_exp,
                     n_tiles):
    step = pl.program_id(0)

    def zero_fills():
        def zero_copy(first_row, n):
            start = pl.multiple_of(first_row * ROW_SLAB, ROW_SLAB)
            return pltpu.make_async_copy(zeros_scr.at[pl.ds(0, n * ROW_SLAB)],
                                         xs_hbm.at[pl.ds(start, n * ROW_SLAB)], zsem)
        fills = []
        for e in range(n_exp):
            row, length = fill_ref[e], fill_ref[n_exp + e]
            for bit in reversed(range(te.bit_length() - 1)):
                take = (length & (1 << bit)) != 0
                fills.append((take, zero_copy(row, 1 << bit)))
                row = row + jnp.where(take, 1 << bit, 0)
        n_used = fill_ref[2 * n_exp]
        for k in range(n_exp):
            fills.append((n_used + k < n_tiles, zero_copy((n_used + k) * te, te)))
        return fills

    @pl.when(step == 0)
    def _():
        zeros_scr[...] = jnp.zeros_like(zeros_scr)
        for take, cp in zero_fills():
            pl.when(take)(cp.start)
        for take, cp in zero_fills():
            pl.when(take)(cp.wait)

    def copies(tok):
        return [_row_copy(h_ref, tok, xs_hbm, pos_ref[0, 0, tok * TOP_K + k], sem)
                for k in range(TOP_K)]

    _for_each(tmd, lambda tok: [cp.start() for cp in copies(tok)])
    _for_each(tmd, lambda tok: [cp.wait() for cp in copies(tok)])


def _moe_dispatch(fill, pos_blocks, h_rows, n_tiles, n_exp, tmd, te):
    n_steps = pos_blocks.shape[0]
    assert te & (te - 1) == 0
    kernel = functools.partial(_dispatch_kernel, tmd=tmd, te=te, n_exp=n_exp, n_tiles=n_tiles)
    return pl.pallas_call(
        kernel,
        out_shape=jax.ShapeDtypeStruct((n_tiles * te * ROW_SLAB, LANES), F32),
        grid_spec=pltpu.PrefetchScalarGridSpec(
            num_scalar_prefetch=1,
            grid=(n_steps,),
            in_specs=[
                pl.BlockSpec((1, 1, tmd * TOP_K), lambda s, cnt: (s, 0, 0), memory_space=pltpu.SMEM),
                pl.BlockSpec((tmd * ROW_SLAB, LANES), lambda s, cnt: (s, 0)),
            ],
            out_specs=pl.BlockSpec(memory_space=pl.ANY),
            scratch_shapes=[pltpu.VMEM((te * ROW_SLAB, LANES), F32), pltpu.SemaphoreType.DMA,
                            pltpu.SemaphoreType.DMA],
        ),
        compiler_params=_cparams(("arbitrary",)),
        name="moe_dispatch",
    )(fill, pos_blocks, h_rows)


def _slab_rows_to_matrix(ref, n_rows, n_chunks):
    return jnp.concatenate([ref[pl.ds(c, n_rows, stride=n_chunks), :] for c in range(n_chunks)],
                           axis=1)


def _experts_kernel(tile_expert_ref, n_used_ref, x_ref, wg_ref, wu_ref, wd_ref, y_ref, *, te,
                    f_chunks):
    del tile_expert_ref
    used = pl.program_id(0) < n_used_ref[0]

    @pl.when(jnp.logical_not(used))
    def _():
        y_ref[...] = jnp.zeros_like(y_ref)

    @pl.when(used)
    def _():
        d = wg_ref.shape[1]
        n_chunks = d // LANES
        x = _slab_rows_to_matrix(x_ref, te, n_chunks).astype(BF16)
        y = jnp.zeros((te, d), F32)
        for lo, hi in f_chunks:
            gate = jnp.dot(x, wg_ref[0, :, lo:hi], preferred_element_type=F32)
            up = jnp.dot(x, wu_ref[0, :, lo:hi], preferred_element_type=F32)
            y = y + jnp.dot((_silu(gate) * up).astype(BF16), wd_ref[0, lo:hi, :],
                            preferred_element_type=F32)
        for c in range(n_chunks):
            y_ref[pl.ds(c, te, stride=n_chunks), :] = y[:, c * LANES:(c + 1) * LANES]


def _f_chunks(f):
    if f <= 1536:
        return ((0, f),)
    half = (f // 2 + 255) // 256 * 256
    return ((0, half), (half, f))


def _moe_experts(tile_expert, n_used, xs, w_gate, w_up, w_down, te):
    n_tiles = tile_expert.shape[0]
    _, d, f = w_gate.shape
    kernel = functools.partial(_experts_kernel, te=te, f_chunks=_f_chunks(f))
    weights = lambda i, te_, nu_: (te_[i], 0, 0)
    return pl.pallas_call(
        kernel,
        out_shape=jax.ShapeDtypeStruct(xs.shape, F32),
        grid_spec=pltpu.PrefetchScalarGridSpec(
            num_scalar_prefetch=2,
            grid=(n_tiles,),
            in_specs=[
                pl.BlockSpec((te * ROW_SLAB, LANES), lambda i, te_, nu_: (jnp.minimum(i, nu_[0] - 1), 0)),
                pl.BlockSpec((1, d, f), weights),
                pl.BlockSpec((1, d, f), weights),
                pl.BlockSpec((1, f, d), weights),
            ],
            out_specs=pl.BlockSpec((te * ROW_SLAB, LANES), lambda i, te_, nu_: (i, 0)),
        ),
        compiler_params=_cparams(("arbitrary",)),
        name="moe_experts",
    )(tile_expert, n_used, xs, w_gate, w_up, w_down)


def _combine_kernel(pos_ref, x_ref, mod_ref, g_ref, wts_ref, ys_hbm, o_ref, buf, sem, *, tmc):
    b = pl.program_id(0)
    d = x_ref.shape[2]
    n_chunks = d // LANES

    def copies(tok):
        return [_row_copy(ys_hbm, pos_ref[0, 0, tok * TOP_K + k], buf, k * tmc + tok, sem)
                for k in range(TOP_K)]

    _for_each(tmc, lambda tok: [cp.start() for cp in copies(tok)])
    _for_each(tmc, lambda tok: [cp.wait() for cp in copies(tok)])

    wts = wts_ref[...]
    y = jnp.zeros((tmc, d), F32)
    for k in range(TOP_K):
        yk = _slab_rows_to_matrix(buf.at[pl.ds(k * tmc * ROW_SLAB, tmc * ROW_SLAB)], tmc, n_chunks)
        y = y + wts[:, k:k + 1] * yk
    _, _, gate = _split_mod(mod_ref, b, d)
    x = x_ref[0] + gate * y
    o_ref[0] = x * lax.rsqrt(jnp.mean(x * x, axis=-1, keepdims=True) + RMS_EPS) * g_ref[...]


def _moe_combine(pos_blocks, x, mods, set_idx, final_g, wts, ys, tmc):
    bsz, s, d = x.shape
    n_i = s // tmc
    kernel = functools.partial(_combine_kernel, tmc=tmc)
    return pl.pallas_call(
        kernel,
        out_shape=jax.ShapeDtypeStruct(x.shape, F32),
        grid=(bsz, n_i),
        in_specs=[
            pl.BlockSpec((1, 1, tmc * TOP_K), lambda b, i: (b * n_i + i, 0, 0),
                         memory_space=pltpu.SMEM),
            pl.BlockSpec((1, tmc, d), lambda b, i: (b, i, 0)),
            pl.BlockSpec((1, SUBLANES, 3 * d), lambda b, i: (set_idx, 0, 0)),
            pl.BlockSpec((1, d), lambda b, i: (0, 0)),
            pl.BlockSpec((tmc, LANES), lambda b, i: (b * n_i + i, 0)),
            pl.BlockSpec(memory_space=pl.ANY),
        ],
        out_specs=pl.BlockSpec((1, tmc, d), lambda b, i: (b, i, 0)),
        scratch_shapes=[pltpu.VMEM((TOP_K * tmc * ROW_SLAB, LANES), F32), pltpu.SemaphoreType.DMA],
        compiler_params=_cparams(("arbitrary", "arbitrary")),
        name="moe_combine",
    )(pos_blocks, x, mods, final_g, wts, ys)


def _pad_group_columns(w, which):
    width = HEADS_PER_GROUP * HEAD_DIM
    n_groups = len(ATTN_GROUPS)
    cols = w[:, which * n_groups * width:(which + 1) * n_groups * width]
    cols = cols.reshape(w.shape[0], n_groups, width)
    return jnp.pad(cols, ((0, 0), (0, 0), (0, GROUP_LANES - width)))


def kernel(x, c, positions, mod_w, mod_b, norm_g, conv_w_in, conv_w, conv_w_out, ffn_w_gate,
           ffn_w_up, ffn_w_down, attn_w_qkv, attn_w_o, router_w, moe_w_gate, moe_w_up, moe_w_down,
           final_g):
    bsz, s, d = x.shape
    t = bsz * s
    n_groups = len(ATTN_GROUPS)
    n_exp = router_w.shape[-1]
    assert all(w // r == ATTN_STEPS for w, r in ATTN_GROUPS)
    assert bsz <= SUBLANES and n_exp <= SUBLANES and d == ROW_SLAB * LANES
    tm = min(512, s)
    assert s % tm == 0 and all(s % (r * ATTN_STEPS) == 0 and tm % r == 0 for _, r in ATTN_GROUPS)

    c_pad = jnp.pad(c.astype(F32), ((0, SUBLANES - bsz), (0, 0)))
    mods = _ada_params(c_pad, mod_w.reshape(-1, d, 3 * d), mod_b.reshape(-1, 1, 3 * d))

    x = _conv_mixer(x, mods, 0, norm_g[0, 0][None], conv_w_in[0].astype(BF16), conv_w[0],
                    conv_w_out[0].astype(BF16), tm)
    x = _dense_ffn(x, mods, 1, norm_g[0, 1][None], ffn_w_gate[0].astype(BF16),
                   ffn_w_up[0].astype(BF16), ffn_w_down[0].astype(BF16), tm)

    w_qkv = attn_w_qkv[0]
    w_groups = jnp.concatenate([_pad_group_columns(w_qkv, which) for which in range(3)], axis=2)
    w_groups = w_groups.transpose(1, 0, 2).astype(BF16)
    inv_freq = ROPE_THETA ** (-jnp.arange(0, HEAD_DIM, 2, dtype=F32) / HEAD_DIM)
    freq = jnp.tile(inv_freq, LANES // (HEAD_DIM // 2))[None]
    qkvs = _qkv_rope(x, mods, 2, norm_g[1, 0][None], positions[..., None], freq, w_groups, tm)
    outs, lses = zip(*[_dilated_attn(qkv, r, tm) for qkv, (_, r) in zip(qkvs, ATTN_GROUPS)])

    width = HEADS_PER_GROUP * HEAD_DIM
    w_o = jnp.pad(attn_w_o[0].reshape(n_groups, width, d), ((0, 0), (0, GROUP_LANES - width), (0, 0)))
    w_o = w_o.reshape(n_groups * GROUP_LANES, d).astype(BF16)
    expand = (jnp.arange(GROUP_LANES)[None, :] // HEAD_DIM == jnp.arange(LANES)[:, None]).astype(BF16)
    w_router = jnp.pad(router_w[0], ((0, 0), (0, LANES - n_exp)))

    x, h_rows, route, wts, counts = _attn_out_route(
        x, mods, 2, 3, norm_g[1, 1][None], outs, lses, expand, w_o, w_router, tm)

    te = min(512, t)
    n_tiles = (t * TOP_K) // te + n_exp
    counts = counts[0, :n_exp]
    tiles_per_expert = (counts + te - 1) // te
    ends = jnp.cumsum(tiles_per_expert)
    first_row = (ends - tiles_per_expert) * te
    n_used = ends[-1:]
    pos_flat = (first_row[route[:, TOP_K:2 * TOP_K]] + route[:, :TOP_K]).reshape(-1)
    fill = jnp.concatenate([first_row + counts, tiles_per_expert * te - counts, n_used]).astype(jnp.int32)
    idx = jnp.minimum(jnp.arange(n_tiles, dtype=jnp.int32), n_used - 1)
    tile_expert = jnp.sum(idx[:, None] >= ends[None, :], axis=1).astype(jnp.int32)

    tmd = min(1024, t)
    xs = _moe_dispatch(fill, pos_flat.reshape(t // tmd, 1, tmd * TOP_K), h_rows, n_tiles, n_exp, tmd, te)
    ys = _moe_experts(tile_expert, n_used.astype(jnp.int32), xs, moe_w_gate[0].astype(BF16),
                      moe_w_up[0].astype(BF16), moe_w_down[0].astype(BF16), te)

    tmc = min(256, s)
    return _moe_combine(pos_flat.reshape(t // tmc, 1, tmc * TOP_K), x, mods, 3, final_g[None], wts,
                        ys, tmc)
```

```python
import functools
import math

import jax
import jax.numpy as jnp
from jax import lax
from jax.experimental import pallas as pl
from jax.experimental.pallas import tpu as pltpu

CONV_WIDTH = 3
ATTN_GROUPS = ((128, 1), (512, 4), (2048, 16))
HEADS_PER_GROUP = 5
HEAD_DIM = 64
ROPE_THETA = 10000.0
TOP_K = 2
RMS_EPS = 1e-6

LANES = 128
SUBLANES = 8
VMEM_LIMIT_BYTES = 56 * 1024 * 1024

GROUP_LANES = 384
PAIRS_PER_GROUP = GROUP_LANES // LANES
ATTN_STEPS = 128
ROW_SLAB = 8

F32 = jnp.float32
BF16 = jnp.bfloat16
NEG_INF = float("-inf")


def _cparams(semantics):
    return pltpu.CompilerParams(dimension_semantics=semantics, vmem_limit_bytes=VMEM_LIMIT_BYTES)


def _norm_mod(x, g, scale, shift):
    y = x * lax.rsqrt(jnp.mean(x * x, axis=-1, keepdims=True) + RMS_EPS)
    return (y * g) * (1.0 + scale) + shift


def _split_mod(mod_ref, b, d):
    row = mod_ref[0, pl.ds(b, 1), :]
    return row[:, :d], row[:, d:2 * d], row[:, 2 * d:]


def _silu(x):
    return x * jax.nn.sigmoid(x)


def _ada_kernel(c_ref, w_ref, b_ref, o_ref):
    s = _silu(c_ref[...])
    o_ref[0] = jnp.dot(s, w_ref[0], precision=lax.Precision.HIGHEST,
                       preferred_element_type=F32) + b_ref[0]


def _ada_params(c_pad, mod_w, mod_b):
    n_sets, d, d3 = mod_w.shape
    tn = d
    return pl.pallas_call(
        _ada_kernel,
        out_shape=jax.ShapeDtypeStruct((n_sets, SUBLANES, d3), F32),
        grid=(n_sets, d3 // tn),
        in_specs=[
            pl.BlockSpec((SUBLANES, d), lambda s, j: (0, 0)),
            pl.BlockSpec((1, d, tn), lambda s, j: (s, 0, j)),
            pl.BlockSpec((1, 1, tn), lambda s, j: (s, 0, j)),
        ],
        out_specs=pl.BlockSpec((1, SUBLANES, tn), lambda s, j: (s, 0, j)),
        compiler_params=_cparams(("parallel", "parallel")),
        name="ada_params",
    )(c_pad, mod_w, mod_b)


def _conv_mixer_kernel(x_ref, mod_ref, g_ref, win_ref, cw_ref, wout_ref, o_ref, carry_ref):
    b, i = pl.program_id(0), pl.program_id(1)
    tm, d = x_ref.shape[1], x_ref.shape[2]

    @pl.when(i == 0)
    def _():
        carry_ref[...] = jnp.zeros_like(carry_ref)

    x = x_ref[0]
    shift, scale, gate = _split_mod(mod_ref, b, d)
    h = _norm_mod(x, g_ref[...], scale, shift).astype(BF16)
    bcu = jnp.dot(h, win_ref[...], preferred_element_type=F32)
    b_gate, v = bcu[:, :d], bcu[:, d:2 * d] * bcu[:, 2 * d:]
    row = lax.broadcasted_iota(jnp.int32, (tm, 1), 0)
    prev = carry_ref[...]
    v1 = jnp.where(row == 0, prev[7:8], pltpu.roll(v, 1, 0))
    v2 = jnp.where(row == 0, prev[6:7], jnp.where(row == 1, prev[7:8], pltpu.roll(v, 2, 0)))
    cw = cw_ref[...]
    conv = cw[0:1] * v2 + cw[1:2] * v1 + cw[2:3] * v
    carry_ref[...] = v[tm - SUBLANES:, :]
    y = jnp.dot((b_gate * conv).astype(BF16), wout_ref[...], preferred_element_type=F32)
    o_ref[0] = x + gate * y


def _conv_mixer(x, mods, set_idx, g, w_in, conv_w, w_out, tm):
    bsz, s, d = x.shape
    return pl.pallas_call(
        _conv_mixer_kernel,
        out_shape=jax.ShapeDtypeStruct(x.shape, F32),
        grid=(bsz, s // tm),
        in_specs=[
            pl.BlockSpec((1, tm, d), lambda b, i: (b, i, 0)),
            pl.BlockSpec((1, SUBLANES, 3 * d), lambda b, i: (set_idx, 0, 0)),
            pl.BlockSpec((1, d), lambda b, i: (0, 0)),
            pl.BlockSpec((d, 3 * d), lambda b, i: (0, 0)),
            pl.BlockSpec((CONV_WIDTH, d), lambda b, i: (0, 0)),
            pl.BlockSpec((d, d), lambda b, i: (0, 0)),
        ],
        out_specs=pl.BlockSpec((1, tm, d), lambda b, i: (b, i, 0)),
        scratch_shapes=[pltpu.VMEM((SUBLANES, d), F32)],
        compiler_params=_cparams(("arbitrary", "arbitrary")),
        name="conv_mixer",
    )(x, mods, g, w_in, conv_w, w_out)


def _ffn_kernel(x_ref, mod_ref, g_ref, wg_ref, wu_ref, wd_ref, o_ref):
    b = pl.program_id(0)
    d = x_ref.shape[2]
    x = x_ref[0]
    shift, scale, gate = _split_mod(mod_ref, b, d)
    h = _norm_mod(x, g_ref[...], scale, shift).astype(BF16)
    a = _silu(jnp.dot(h, wg_ref[...], preferred_element_type=F32)) * jnp.dot(
        h, wu_ref[...], preferred_element_type=F32)
    y = jnp.dot(a.astype(BF16), wd_ref[...], preferred_element_type=F32)
    o_ref[0] = x + gate * y


def _dense_ffn(x, mods, set_idx, g, w_gate, w_up, w_down, tm):
    bsz, s, d = x.shape
    f = w_gate.shape[1]
    resident = pl.Buffered(1)
    return pl.pallas_call(
        _ffn_kernel,
        out_shape=jax.ShapeDtypeStruct(x.shape, F32),
        grid=(bsz, s // tm),
        in_specs=[
            pl.BlockSpec((1, tm, d), lambda b, i: (b, i, 0)),
            pl.BlockSpec((1, SUBLANES, 3 * d), lambda b, i: (set_idx, 0, 0)),
            pl.BlockSpec((1, d), lambda b, i: (0, 0)),
            pl.BlockSpec((d, f), lambda b, i: (0, 0), pipeline_mode=resident),
            pl.BlockSpec((d, f), lambda b, i: (0, 0), pipeline_mode=resident),
            pl.BlockSpec((f, d), lambda b, i: (0, 0), pipeline_mode=resident),
        ],
        out_specs=pl.BlockSpec((1, tm, d), lambda b, i: (b, i, 0)),
        compiler_params=_cparams(("parallel", "parallel")),
        name="dense_ffn",
    )(x, mods, g, w_gate, w_up, w_down)


def _class_major(ref, tm, r):
    n_chunks = ref.shape[0]
    if r == 1:
        return jnp.concatenate([ref[c] for c in range(n_chunks)], axis=1)
    return jnp.concatenate(
        [jnp.concatenate([ref[c, pl.ds(res, tm // r, stride=r), :] for c in range(n_chunks)], axis=1)
         for res in range(r)], axis=0)


def _store_lane_chunks(ref, val):
    for c in range(ref.shape[0]):
        ref[c] = val[:, c * LANES:(c + 1) * LANES]


def _rope(t, cos, sin_signed, first_half):
    out = []
    for c in range(t.shape[1] // LANES):
        tc = t[:, c * LANES:(c + 1) * LANES]
        rot = jnp.where(first_half, pltpu.roll(tc, LANES - HEAD_DIM // 2, 1),
                        pltpu.roll(tc, HEAD_DIM // 2, 1))
        out.append(tc * cos + rot * sin_signed)
    return jnp.concatenate(out, axis=1)


def _qkv_kernel(x_ref, mod_ref, g_ref, pos_ref, freq_ref, w_ref, o0_ref, o1_ref, o2_ref,
                h_scr, cos_scr, sin_scr):
    b = pl.program_id(0)
    tm, d = x_ref.shape[1], x_ref.shape[2]
    shift, scale, _ = _split_mod(mod_ref, b, d)
    _store_lane_chunks(h_scr, _norm_mod(x_ref[0], g_ref[...], scale, shift))
    ang = pos_ref[0].astype(F32) * freq_ref[...]
    first_half = (lax.broadcasted_iota(jnp.int32, (1, LANES), 1) % HEAD_DIM) < HEAD_DIM // 2
    cos_scr[0] = jnp.cos(ang)
    sin_scr[0] = jnp.where(first_half, -jnp.sin(ang), jnp.sin(ang))
    gl = GROUP_LANES
    for gi, ((_, r), o_ref) in enumerate(zip(ATTN_GROUPS, (o0_ref, o1_ref, o2_ref))):
        hg = _class_major(h_scr, tm, r).astype(BF16)
        cos, sin_signed = _class_major(cos_scr, tm, r), _class_major(sin_scr, tm, r)
        qkv = jnp.dot(hg, w_ref[gi], preferred_element_type=F32)
        q = _rope(qkv[:, :gl], cos, sin_signed, first_half) * (HEAD_DIM ** -0.5)
        k = _rope(qkv[:, gl:2 * gl], cos, sin_signed, first_half)
        o_ref[0, :, :gl] = q.astype(BF16)
        o_ref[0, :, gl:2 * gl] = k.astype(BF16)
        o_ref[0, :, 2 * gl:] = qkv[:, 2 * gl:].astype(BF16)


def _qkv_rope(x, mods, set_idx, g, pos3, freq, w_groups, tm):
    bsz, s, d = x.shape
    n_groups = len(ATTN_GROUPS)
    out = jax.ShapeDtypeStruct((bsz, s, 3 * GROUP_LANES), BF16)
    return pl.pallas_call(
        _qkv_kernel,
        out_shape=[out] * n_groups,
        grid=(bsz, s // tm),
        in_specs=[
            pl.BlockSpec((1, tm, d), lambda b, i: (b, i, 0)),
            pl.BlockSpec((1, SUBLANES, 3 * d), lambda b, i: (set_idx, 0, 0)),
            pl.BlockSpec((1, d), lambda b, i: (0, 0)),
            pl.BlockSpec((1, tm, 1), lambda b, i: (b, i, 0)),
            pl.BlockSpec((1, LANES), lambda b, i: (0, 0)),
            pl.BlockSpec((n_groups, d, 3 * GROUP_LANES), lambda b, i: (0, 0, 0)),
        ],
        out_specs=[pl.BlockSpec((1, tm, 3 * GROUP_LANES), lambda b, i: (b, i, 0))] * n_groups,
        scratch_shapes=[pltpu.VMEM((d // LANES, tm, LANES), F32), pltpu.VMEM((1, tm, LANES), F32),
                        pltpu.VMEM((1, tm, LANES), F32)],
        compiler_params=_cparams(("parallel", "parallel")),
        name="qkv_rope",
    )(x, mods, g, pos3, freq, w_groups)


def _store_rows(ref, a, val):
    st, per_tile = ATTN_STEPS, ref.shape[1]
    if per_tile >= st:
        ref[(a * st) // per_tile, pl.ds((a * st) % per_tile, st), :] = val
    else:
        for j in range(st // per_tile):
            ref[a * (st // per_tile) + j] = val[j * per_tile:(j + 1) * per_tile]


def _attn_kernel(q_ref, kc_ref, vc_ref, kp_ref, vp_ref, o_ref, lse_ref):
    i = pl.program_id(2)
    st = ATTN_STEPS
    tq = q_ref.shape[0] * q_ref.shape[1]
    q = q_ref[...].reshape(tq, GROUP_LANES)
    k_rows = jnp.concatenate([kp_ref[...].reshape(st, GROUP_LANES),
                              kc_ref[...].reshape(tq, GROUP_LANES)], axis=0)
    v_rows = jnp.concatenate([vp_ref[...].reshape(st, GROUP_LANES),
                              vc_ref[...].reshape(tq, GROUP_LANES)], axis=0)
    lane = lax.broadcasted_iota(jnp.int32, (1, LANES), 1)
    qi = lax.broadcasted_iota(jnp.int32, (st, 2 * st), 0)
    kj = lax.broadcasted_iota(jnp.int32, (st, 2 * st), 1)
    band = (kj >= qi) & (kj <= qi + st)
    band_first = band & ((kj >= st) | (i > 0))
    nt_dims = (((1,), (1,)), ((), ()))
    zero = jnp.zeros((), BF16)

    k_heads, v_heads = [], []
    for pair in range(PAIRS_PER_GROUP):
        sl = slice(pair * LANES, (pair + 1) * LANES)
        kp_, vp_ = k_rows[:, sl], v_rows[:, sl]
        if HEADS_PER_GROUP - 2 * pair >= 2:
            halves = [lane < HEAD_DIM, lane >= HEAD_DIM]
            k_heads.append([jnp.where(m, kp_, zero) for m in halves])
            v_heads.append([jnp.where(m, vp_, zero) for m in halves])
        else:
            k_heads.append([kp_])
            v_heads.append([vp_])

    blocks = [(a, pair, hh) for a in range(tq // st) for pair in range(PAIRS_PER_GROUP)
              for hh in range(len(k_heads[pair]))]
    scores = []
    for a, pair, hh in blocks:
        qa = q[a * st:(a + 1) * st, pair * LANES:(pair + 1) * LANES]
        s = lax.dot_general(qa, k_heads[pair][hh][a * st:(a + 2) * st], nt_dims,
                            preferred_element_type=F32)
        scores.append(jnp.where(band if a else band_first, s, NEG_INF))
    s_all = jnp.concatenate(scores, axis=0)
    m_all = jnp.max(s_all, axis=-1, keepdims=True)
    p_all = jnp.exp(s_all - m_all)
    l_all = jnp.sum(p_all, axis=-1, keepdims=True)
    p_all = p_all.astype(BF16)
    inv_all = 1.0 / l_all
    lse_all = m_all + jnp.log(l_all)

    for a in range(tq // st):
        lse_tile = jnp.zeros((st, LANES), F32)
        o_tiles = []
        for pair in range(PAIRS_PER_GROUP):
            acc = jnp.zeros((st, LANES), F32)
            for hh in range(len(k_heads[pair])):
                rows = slice(blocks.index((a, pair, hh)) * st, (blocks.index((a, pair, hh)) + 1) * st)
                pv = jnp.dot(p_all[rows], v_heads[pair][hh][a * st:(a + 2) * st],
                             preferred_element_type=F32)
                acc = acc + pv * inv_all[rows]
                lse_tile = jnp.where(lane == 2 * pair + hh, lse_all[rows], lse_tile)
            o_tiles.append(acc.astype(BF16))
        _store_rows(o_ref, a, jnp.concatenate(o_tiles, axis=1))
        _store_rows(lse_ref, a, lse_tile)


def _dilated_attn(qkv, r, tmp):
    bsz, s, width = qkv.shape
    gl, st = GROUP_LANES, ATTN_STEPS
    cls_rows = tmp // r
    tq = min(512, s // r)
    nt = tq // cls_rows
    n_tiles = s // tmp
    n_q = (s // r) // tq
    sub = tq // st
    main = qkv.reshape(bsz, n_tiles, r, cls_rows, width)

    def main_spec(col):
        return pl.BlockSpec((None, nt, None, cls_rows, gl), lambda b, res, i: (b, i, res, 0, col))

    if cls_rows >= st:
        per_tile = cls_rows // st
        prev_arr = qkv.reshape(bsz, n_tiles, r, per_tile, st, width)

        def prev_spec(col):
            def index(b, res, i):
                c = jnp.maximum(i * sub - 1, 0)
                return (b, c // per_tile, res, c % per_tile, 0, col)
            return pl.BlockSpec((None, None, None, None, st, gl), index)
    else:
        ntp = st // cls_rows
        prev_arr = qkv.reshape(bsz, n_tiles // ntp, ntp, r, cls_rows, width)

        def prev_spec(col):
            return pl.BlockSpec((None, None, ntp, None, cls_rows, gl),
                                lambda b, res, i: (b, jnp.maximum(i * sub - 1, 0), 0, res, 0, col))

    o5 = jax.ShapeDtypeStruct((bsz, n_tiles, r, cls_rows, gl), BF16)
    l5 = jax.ShapeDtypeStruct((bsz, n_tiles, r, cls_rows, LANES), F32)

    o, lse = pl.pallas_call(
        _attn_kernel,
        out_shape=[o5, l5],
        grid=(bsz, r, n_q),
        in_specs=[main_spec(0), main_spec(1), main_spec(2), prev_spec(1), prev_spec(2)],
        out_specs=[
            pl.BlockSpec((None, nt, None, cls_rows, gl), lambda b, res, i: (b, i, res, 0, 0)),
            pl.BlockSpec((None, nt, None, cls_rows, LANES), lambda b, res, i: (b, i, res, 0, 0)),
        ],
        compiler_params=_cparams(("parallel", "parallel", "parallel")),
        name=f"dilated_attn_r{r}",
    )(main, main, main, prev_arr, prev_arr)
    return o.reshape(bsz, s, gl), lse.reshape(bsz, s, LANES)


def _token_major(scr, val, tm, r):
    if r == 1:
        return val
    n = tm // r
    for c in range(scr.shape[0]):
        for res in range(r):
            scr[c, pl.ds(res, n, stride=r), :] = val[res * n:(res + 1) * n, c * LANES:(c + 1) * LANES]
    return jnp.concatenate([scr[c] for c in range(scr.shape[0])], axis=1)


def _attn_out_route_kernel(x_ref, mod_a_ref, mod_m_ref, g_ref, o0_ref, o1_ref, o2_ref,
                           l0_ref, l1_ref, l2_ref, expand_ref, wo_ref, wr_ref,
                           x_out_ref, h_out_ref, route_ref, wts_ref, cnt_ref,
                           o_scr, l_scr, base_scr):
    b, i = pl.program_id(0), pl.program_id(1)
    tm, d = x_ref.shape[1], x_ref.shape[2]
    n_exp = cnt_ref.shape[0]

    @pl.when((b == 0) & (i == 0))
    def _():
        base_scr[...] = jnp.zeros_like(base_scr)

    o_refs, l_refs = (o0_ref, o1_ref, o2_ref), (l0_ref, l1_ref, l2_ref)
    outs, lses = [], []
    for gi, (_, r) in enumerate(ATTN_GROUPS):
        lses.append(_token_major(l_scr.at[gi], l_refs[gi][0], tm, r))
        outs.append(_token_major(o_scr.at[gi], o_refs[gi][0].astype(F32), tm, r))
    m = jnp.maximum(jnp.maximum(lses[0], lses[1]), lses[2])
    es = [jnp.exp(l - m) for l in lses]
    inv = 1.0 / (es[0] + es[1] + es[2])
    scaled = []
    for gi in range(len(ATTN_GROUPS)):
        alpha = es[gi] * inv
        hi = alpha.astype(BF16)
        lo = (alpha - hi.astype(F32)).astype(BF16)
        wide = jnp.dot(hi, expand_ref[...], preferred_element_type=F32) + jnp.dot(
            lo, expand_ref[...], preferred_element_type=F32)
        scaled.append((outs[gi] * wide).astype(BF16))
    attn = jnp.dot(jnp.concatenate(scaled, axis=1), wo_ref[...], preferred_element_type=F32)
    _, _, gate_a = _split_mod(mod_a_ref, b, d)
    x = x_ref[0] + gate_a * attn
    x_out_ref[0] = x

    shift, scale, _ = _split_mod(mod_m_ref, b, d)
    h = _norm_mod(x, g_ref[...], scale, shift)
    for c in range(d // LANES):
        h_out_ref[pl.ds(c, tm, stride=d // LANES), :] = h[:, c * LANES:(c + 1) * LANES]

    lane = lax.broadcasted_iota(jnp.int32, (1, LANES), 1)
    h_hi = h.astype(BF16)
    h_lo = (h - h_hi.astype(F32)).astype(BF16)
    logits = (jnp.dot(h_hi, wr_ref[0], preferred_element_type=F32)
              + jnp.dot(h_lo, wr_ref[0], preferred_element_type=F32)
              + jnp.dot(h_hi, wr_ref[1], preferred_element_type=F32))
    logits = jnp.where(lane < n_exp, logits, NEG_INF)
    m1 = jnp.max(logits, axis=-1, keepdims=True)
    i1 = jnp.min(jnp.where(logits == m1, lane, LANES), axis=-1, keepdims=True)
    rest = jnp.where(lane == i1, NEG_INF, logits)
    m2 = jnp.max(rest, axis=-1, keepdims=True)
    i2 = jnp.min(jnp.where(rest == m2, lane, LANES), axis=-1, keepdims=True)
    e2 = jnp.exp(m2 - m1)
    w1 = 1.0 / (1.0 + e2)
    w2 = e2 * w1
    wts_ref[...] = jnp.where(lane == 0, w1, jnp.where(lane == 1, w2, 0.0))

    oh1, oh2 = (lane == i1), (lane == i2)
    picks = jnp.where(oh1 | oh2, 1.0, 0.0).astype(BF16)
    ri = lax.broadcasted_iota(jnp.int32, (tm, tm), 0)
    ci = lax.broadcasted_iota(jnp.int32, (tm, tm), 1)
    before = jnp.where(ci < ri, 1.0, 0.0).astype(BF16)
    rank = jnp.dot(before, picks, preferred_element_type=F32) + base_scr[0:1, :]
    r1 = jnp.sum(jnp.where(oh1, rank, 0.0), axis=-1, keepdims=True).astype(jnp.int32)
    r2 = jnp.sum(jnp.where(oh2, rank, 0.0), axis=-1, keepdims=True).astype(jnp.int32)
    route_ref[...] = jnp.where(lane == 0, r1, jnp.where(lane == 1, r2, jnp.where(
        lane == 2, i1, jnp.where(lane == 3, i2, 0))))
    total = base_scr[0:1, :] + jnp.sum(picks.astype(F32), axis=0, keepdims=True)
    base_scr[...] = jnp.broadcast_to(total, base_scr.shape)
    cnt_ref[...] = jnp.broadcast_to(total, cnt_ref.shape).astype(jnp.int32)


def _attn_out_route(x, mods, set_a, set_m, g, outs, lses, expand, w_o, w_router, tm):
    bsz, s, d = x.shape
    t = bsz * s
    n_i = s // tm
    n_groups = len(ATTN_GROUPS)
    gl = GROUP_LANES
    tok = lambda b, i: (b, i, 0)
    flat = lambda b, i: (b * n_i + i, 0)
    const2 = lambda b, i: (0, 0)
    return pl.pallas_call(
        _attn_out_route_kernel,
        out_shape=[
            jax.ShapeDtypeStruct((bsz, s, d), F32),
            jax.ShapeDtypeStruct((t * ROW_SLAB, LANES), F32),
            jax.ShapeDtypeStruct((t, LANES), jnp.int32),
            jax.ShapeDtypeStruct((t, LANES), F32),
            jax.ShapeDtypeStruct((SUBLANES, LANES), jnp.int32),
        ],
        grid=(bsz, n_i),
        in_specs=[
            pl.BlockSpec((1, tm, d), tok),
            pl.BlockSpec((1, SUBLANES, 3 * d), lambda b, i: (set_a, 0, 0)),
            pl.BlockSpec((1, SUBLANES, 3 * d), lambda b, i: (set_m, 0, 0)),
            pl.BlockSpec((1, d), const2),
        ] + [pl.BlockSpec((1, tm, gl), tok)] * n_groups
          + [pl.BlockSpec((1, tm, LANES), tok)] * n_groups + [
            pl.BlockSpec((LANES, gl), const2),
            pl.BlockSpec((n_groups * gl, d), const2),
            pl.BlockSpec((2, d, LANES), lambda b, i: (0, 0, 0)),
        ],
        out_specs=[
            pl.BlockSpec((1, tm, d), tok),
            pl.BlockSpec((tm * ROW_SLAB, LANES), flat),
            pl.BlockSpec((tm, LANES), flat),
            pl.BlockSpec((tm, LANES), flat),
            pl.BlockSpec((SUBLANES, LANES), const2),
        ],
        scratch_shapes=[pltpu.VMEM((n_groups, gl // LANES, tm, LANES), F32),
                        pltpu.VMEM((n_groups, 1, tm, LANES), F32),
                        pltpu.VMEM((SUBLANES, LANES), F32)],
        compiler_params=_cparams(("arbitrary", "arbitrary")),
        name="attn_out_route",
    )(x, mods, mods, g, *outs, *lses, expand, w_o, w_router)


DMA_LOOP_UNROLL = 16


def _for_each(n, fn):
    def body(g, carry):
        for u in range(DMA_LOOP_UNROLL):
            fn(g * DMA_LOOP_UNROLL + u)
        return carry
    lax.fori_loop(0, n // DMA_LOOP_UNROLL, body, 0)


def _row_copy(src, src_row, dst, dst_row, sem):
    return pltpu.make_async_copy(src.at[pl.ds(pl.multiple_of(src_row * ROW_SLAB, ROW_SLAB), ROW_SLAB)],
                                 dst.at[pl.ds(pl.multiple_of(dst_row * ROW_SLAB, ROW_SLAB), ROW_SLAB)],
                                 sem)


def _dispatch_kernel(fill_ref, pos_ref, h_ref, xs_hbm, zeros_scr, sem, zsem, *, tmd, te, n_exp,
                     n_tiles):
    step = pl.program_id(0)

    def zero_fills():
        def zero_copy(first_row, n):
            start = pl.multiple_of(first_row * ROW_SLAB, ROW_SLAB)
            return pltpu.make_async_copy(zeros_scr.at[pl.ds(0, n * ROW_SLAB)],
                                         xs_hbm.at[pl.ds(start, n * ROW_SLAB)], zsem)
        fills = []
        for e in range(n_exp):
            row, length = fill_ref[e], fill_ref[n_exp + e]
            for bit in reversed(range(te.bit_length() - 1)):
                take = (length & (1 << bit)) != 0
                fills.append((take, zero_copy(row, 1 << bit)))
                row = row + jnp.where(take, 1 << bit, 0)
        n_used = fill_ref[2 * n_exp]
        for k in range(n_exp):
            fills.append((n_used + k < n_tiles, zero_copy((n_used + k) * te, te)))
        return fills

    @pl.when(step == 0)
    def _():
        zeros_scr[...] = jnp.zeros_like(zeros_scr)
        for take, cp in zero_fills():
            pl.when(take)(cp.start)
        for take, cp in zero_fills():
            pl.when(take)(cp.wait)

    def copies(tok):
        return [_row_copy(h_ref, tok, xs_hbm, pos_ref[0, 0, tok * TOP_K + k], sem)
                for k in range(TOP_K)]

    _for_each(tmd, lambda tok: [cp.start() for cp in copies(tok)])
    _for_each(tmd, lambda tok: [cp.wait() for cp in copies(tok)])


def _moe_dispatch(fill, pos_blocks, h_rows, n_tiles, n_exp, tmd, te):
    n_steps = pos_blocks.shape[0]
    assert te & (te - 1) == 0
    kernel = functools.partial(_dispatch_kernel, tmd=tmd, te=te, n_exp=n_exp, n_tiles=n_tiles)
    return pl.pallas_call(
        kernel,
        out_shape=jax.ShapeDtypeStruct((n_tiles * te * ROW_SLAB, LANES), F32),
        grid_spec=pltpu.PrefetchScalarGridSpec(
            num_scalar_prefetch=1,
            grid=(n_steps,),
            in_specs=[
                pl.BlockSpec((1, 1, tmd * TOP_K), lambda s, cnt: (s, 0, 0), memory_space=pltpu.SMEM),
                pl.BlockSpec((tmd * ROW_SLAB, LANES), lambda s, cnt: (s, 0)),
            ],
            out_specs=pl.BlockSpec(memory_space=pl.ANY),
            scratch_shapes=[pltpu.VMEM((te * ROW_SLAB, LANES), F32), pltpu.SemaphoreType.DMA,
                            pltpu.SemaphoreType.DMA],
        ),
        compiler_params=_cparams(("arbitrary",)),
        name="moe_dispatch",
    )(fill, pos_blocks, h_rows)


def _slab_rows_to_matrix(ref, n_rows, n_chunks):
    return jnp.concatenate([ref[pl.ds(c, n_rows, stride=n_chunks), :] for c in range(n_chunks)],
                           axis=1)


def _experts_kernel(tile_expert_ref, n_used_ref, x_ref, wg_ref, wu_ref, wd_ref, y_ref, *, te,
                    f_chunks):
    del tile_expert_ref
    used = pl.program_id(0) < n_used_ref[0]

    @pl.when(jnp.logical_not(used))
    def _():
        y_ref[...] = jnp.zeros_like(y_ref)

    @pl.when(used)
    def _():
        d = wg_ref.shape[1]
        n_chunks = d // LANES
        x = _slab_rows_to_matrix(x_ref, te, n_chunks).astype(BF16)
        y = jnp.zeros((te, d), F32)
        for lo, hi in f_chunks:
            gate = jnp.dot(x, wg_ref[0, :, lo:hi], preferred_element_type=F32)
            up = jnp.dot(x, wu_ref[0, :, lo:hi], preferred_element_type=F32)
            y = y + jnp.dot((_silu(gate) * up).astype(BF16), wd_ref[0, lo:hi, :],
                            preferred_element_type=F32)
        for c in range(n_chunks):
            y_ref[pl.ds(c, te, stride=n_chunks), :] = y[:, c * LANES:(c + 1) * LANES]


def _f_chunks(f):
    if f <= 1536:
        return ((0, f),)
    half = (f // 2 + 255) // 256 * 256
    return ((0, half), (half, f))


def _moe_experts(tile_expert, n_used, xs, w_gate, w_up, w_down, te):
    n_tiles = tile_expert.shape[0]
    _, d, f = w_gate.shape
    kernel = functools.partial(_experts_kernel, te=te, f_chunks=_f_chunks(f))
    weights = lambda i, te_, nu_: (te_[i], 0, 0)
    return pl.pallas_call(
        kernel,
        out_shape=jax.ShapeDtypeStruct(xs.shape, F32),
        grid_spec=pltpu.PrefetchScalarGridSpec(
            num_scalar_prefetch=2,
            grid=(n_tiles,),
            in_specs=[
                pl.BlockSpec((te * ROW_SLAB, LANES), lambda i, te_, nu_: (jnp.minimum(i, nu_[0] - 1), 0)),
                pl.BlockSpec((1, d, f), weights),
                pl.BlockSpec((1, d, f), weights),
                pl.BlockSpec((1, f, d), weights),
            ],
            out_specs=pl.BlockSpec((te * ROW_SLAB, LANES), lambda i, te_, nu_: (i, 0)),
        ),
        compiler_params=_cparams(("arbitrary",)),
        name="moe_experts",
    )(tile_expert, n_used, xs, w_gate, w_up, w_down)


def _combine_kernel(pos_ref, pos_next_ref, x_ref, mod_ref, g_ref, wts_ref, ys_hbm, o_ref, buf, sems,
                    *, tmc):
    b = pl.program_id(0)
    step = b * pl.num_programs(1) + pl.program_id(1)
    n_steps = pl.num_programs(0) * pl.num_programs(1)
    slot = step % 2
    d = x_ref.shape[2]
    n_chunks = d // LANES

    def copies(p_ref, slot_, tok):
        return [_row_copy(ys_hbm, p_ref[0, 0, tok * TOP_K + k], buf.at[slot_], k * tmc + tok,
                          sems.at[slot_]) for k in range(TOP_K)]

    def fetch(p_ref, slot_):
        _for_each(tmc, lambda tok: [cp.start() for cp in copies(p_ref, slot_, tok)])

    pl.when(step == 0)(lambda: fetch(pos_ref, slot))
    pl.when(step + 1 < n_steps)(lambda: fetch(pos_next_ref, 1 - slot))
    _for_each(tmc, lambda tok: [cp.wait() for cp in copies(pos_ref, slot, tok)])

    wts = wts_ref[...]
    y = jnp.zeros((tmc, d), F32)
    rows = buf.at[slot]
    for k in range(TOP_K):
        yk = _slab_rows_to_matrix(rows.at[pl.ds(k * tmc * ROW_SLAB, tmc * ROW_SLAB)], tmc, n_chunks)
        y = y + wts[:, k:k + 1] * yk
    _, _, gate = _split_mod(mod_ref, b, d)
    x = x_ref[0] + gate * y
    o_ref[0] = x * lax.rsqrt(jnp.mean(x * x, axis=-1, keepdims=True) + RMS_EPS) * g_ref[...]


def _moe_combine(pos_blocks, x, mods, set_idx, final_g, wts, ys, tmc):
    bsz, s, d = x.shape
    n_i = s // tmc
    last = bsz * n_i - 1
    kernel = functools.partial(_combine_kernel, tmc=tmc)
    return pl.pallas_call(
        kernel,
        out_shape=jax.ShapeDtypeStruct(x.shape, F32),
        grid=(bsz, n_i),
        in_specs=[
            pl.BlockSpec((1, 1, tmc * TOP_K), lambda b, i: (b * n_i + i, 0, 0),
                         memory_space=pltpu.SMEM),
            pl.BlockSpec((1, 1, tmc * TOP_K), lambda b, i: (jnp.minimum(b * n_i + i + 1, last), 0, 0),
                         memory_space=pltpu.SMEM),
            pl.BlockSpec((1, tmc, d), lambda b, i: (b, i, 0)),
            pl.BlockSpec((1, SUBLANES, 3 * d), lambda b, i: (set_idx, 0, 0)),
            pl.BlockSpec((1, d), lambda b, i: (0, 0)),
            pl.BlockSpec((tmc, LANES), lambda b, i: (b * n_i + i, 0)),
            pl.BlockSpec(memory_space=pl.ANY),
        ],
        out_specs=pl.BlockSpec((1, tmc, d), lambda b, i: (b, i, 0)),
        scratch_shapes=[pltpu.VMEM((2, TOP_K * tmc * ROW_SLAB, LANES), F32),
                        pltpu.SemaphoreType.DMA((2,))],
        compiler_params=_cparams(("arbitrary", "arbitrary")),
        name="moe_combine",
    )(pos_blocks, pos_blocks, x, mods, final_g, wts, ys)


def _pad_group_columns(w, which):
    width = HEADS_PER_GROUP * HEAD_DIM
    n_groups = len(ATTN_GROUPS)
    cols = w[:, which * n_groups * width:(which + 1) * n_groups * width]
    cols = cols.reshape(w.shape[0], n_groups, width)
    return jnp.pad(cols, ((0, 0), (0, 0), (0, GROUP_LANES - width)))


def kernel(x, c, positions, mod_w, mod_b, norm_g, conv_w_in, conv_w, conv_w_out, ffn_w_gate,
           ffn_w_up, ffn_w_down, attn_w_qkv, attn_w_o, router_w, moe_w_gate, moe_w_up, moe_w_down,
           final_g):
    bsz, s, d = x.shape
    t = bsz * s
    n_groups = len(ATTN_GROUPS)
    n_exp = router_w.shape[-1]
    assert all(w // r == ATTN_STEPS for w, r in ATTN_GROUPS)
    assert bsz <= SUBLANES and n_exp <= SUBLANES and d == ROW_SLAB * LANES
    tm = min(512, s)
    assert s % tm == 0 and all(s % (r * ATTN_STEPS) == 0 and tm % r == 0 for _, r in ATTN_GROUPS)

    c_pad = jnp.pad(c.astype(F32), ((0, SUBLANES - bsz), (0, 0)))
    mods = _ada_params(c_pad, mod_w.reshape(-1, d, 3 * d), mod_b.reshape(-1, 1, 3 * d))

    x = _conv_mixer(x, mods, 0, norm_g[0, 0][None], conv_w_in[0].astype(BF16), conv_w[0],
                    conv_w_out[0].astype(BF16), tm)
    x = _dense_ffn(x, mods, 1, norm_g[0, 1][None], ffn_w_gate[0].astype(BF16),
                   ffn_w_up[0].astype(BF16), ffn_w_down[0].astype(BF16), tm)

    w_qkv = attn_w_qkv[0]
    w_groups = jnp.concatenate([_pad_group_columns(w_qkv, which) for which in range(3)], axis=2)
    w_groups = w_groups.transpose(1, 0, 2).astype(BF16)
    inv_freq = ROPE_THETA ** (-jnp.arange(0, HEAD_DIM, 2, dtype=F32) / HEAD_DIM)
    freq = jnp.tile(inv_freq, LANES // (HEAD_DIM // 2))[None]
    qkvs = _qkv_rope(x, mods, 2, norm_g[1, 0][None], positions[..., None], freq, w_groups, tm)
    outs, lses = zip(*[_dilated_attn(qkv, r, tm) for qkv, (_, r) in zip(qkvs, ATTN_GROUPS)])

    width = HEADS_PER_GROUP * HEAD_DIM
    w_o = jnp.pad(attn_w_o[0].reshape(n_groups, width, d), ((0, 0), (0, GROUP_LANES - width), (0, 0)))
    w_o = w_o.reshape(n_groups * GROUP_LANES, d).astype(BF16)
    expand = (jnp.arange(GROUP_LANES)[None, :] // HEAD_DIM == jnp.arange(LANES)[:, None]).astype(BF16)
    w_router = jnp.pad(router_w[0], ((0, 0), (0, LANES - n_exp)))
    w_router_hi = w_router.astype(BF16)
    w_router = jnp.stack([w_router_hi, (w_router - w_router_hi.astype(F32)).astype(BF16)])

    x, h_rows, route, wts, counts = _attn_out_route(
        x, mods, 2, 3, norm_g[1, 1][None], outs, lses, expand, w_o, w_router, tm)

    te = min(512, t)
    n_tiles = (t * TOP_K) // te + n_exp
    counts = counts[0, :n_exp]
    tiles_per_expert = (counts + te - 1) // te
    ends = jnp.cumsum(tiles_per_expert)
    first_row = (ends - tiles_per_expert) * te
    n_used = ends[-1:]
    pos_flat = (first_row[route[:, TOP_K:2 * TOP_K]] + route[:, :TOP_K]).reshape(-1)
    fill = jnp.concatenate([first_row + counts, tiles_per_expert * te - counts, n_used]).astype(jnp.int32)
    idx = jnp.minimum(jnp.arange(n_tiles, dtype=jnp.int32), n_used - 1)
    tile_expert = jnp.sum(idx[:, None] >= ends[None, :], axis=1).astype(jnp.int32)

    tmd = min(1024, t)
    xs = _moe_dispatch(fill, pos_flat.reshape(t // tmd, 1, tmd * TOP_K), h_rows, n_tiles, n_exp, tmd, te)
    ys = _moe_experts(tile_expert, n_used.astype(jnp.int32), xs, moe_w_gate[0].astype(BF16),
                      moe_w_up[0].astype(BF16), moe_w_down[0].astype(BF16), te)

    tmc = min(512, s)
    return _moe_combine(pos_flat.reshape(t // tmc, 1, tmc * TOP_K), x, mods, 3, final_g[None], wts,
                        ys, tmc)
```

```python
import functools
import math

import jax
import jax.numpy as jnp
from jax import lax
from jax.experimental import pallas as pl
from jax.experimental.pallas import tpu as pltpu

CONV_WIDTH = 3
ATTN_GROUPS = ((128, 1), (512, 4), (2048, 16))
HEADS_PER_GROUP = 5
HEAD_DIM = 64
ROPE_THETA = 10000.0
TOP_K = 2
RMS_EPS = 1e-6

LANES = 128
SUBLANES = 8
VMEM_LIMIT_BYTES = 56 * 1024 * 1024

GROUP_LANES = 384
PAIRS_PER_GROUP = GROUP_LANES // LANES
ATTN_STEPS = 128
ROW_SLAB = 8

F32 = jnp.float32
BF16 = jnp.bfloat16
NEG_INF = float("-inf")


def _cparams(semantics):
    return pltpu.CompilerParams(dimension_semantics=semantics, vmem_limit_bytes=VMEM_LIMIT_BYTES)


def _norm_mod(x, g, scale, shift):
    y = x * lax.rsqrt(jnp.mean(x * x, axis=-1, keepdims=True) + RMS_EPS)
    return (y * g) * (1.0 + scale) + shift


def _split_mod(mod_ref, b, d):
    row = mod_ref[0, pl.ds(b, 1), :]
    return row[:, :d], row[:, d:2 * d], row[:, 2 * d:]


def _silu(x):
    return x * jax.nn.sigmoid(x)


def _ada_kernel(c_ref, w_ref, b_ref, o_ref):
    s = _silu(c_ref[...])
    o_ref[0] = jnp.dot(s, w_ref[0], precision=lax.Precision.HIGHEST,
                       preferred_element_type=F32) + b_ref[0]


def _ada_params(c_pad, mod_w, mod_b):
    n_sets, d, d3 = mod_w.shape
    tn = d
    return pl.pallas_call(
        _ada_kernel,
        out_shape=jax.ShapeDtypeStruct((n_sets, SUBLANES, d3), F32),
        grid=(n_sets, d3 // tn),
        in_specs=[
            pl.BlockSpec((SUBLANES, d), lambda s, j: (0, 0)),
            pl.BlockSpec((1, d, tn), lambda s, j: (s, 0, j)),
            pl.BlockSpec((1, 1, tn), lambda s, j: (s, 0, j)),
        ],
        out_specs=pl.BlockSpec((1, SUBLANES, tn), lambda s, j: (s, 0, j)),
        compiler_params=_cparams(("parallel", "parallel")),
        name="ada_params",
    )(c_pad, mod_w, mod_b)


def _conv_mixer_kernel(x_ref, mod_ref, g_ref, win_ref, cw_ref, wout_ref, o_ref, carry_ref):
    b, i = pl.program_id(0), pl.program_id(1)
    tm, d = x_ref.shape[1], x_ref.shape[2]

    @pl.when(i == 0)
    def _():
        carry_ref[...] = jnp.zeros_like(carry_ref)

    x = x_ref[0]
    shift, scale, gate = _split_mod(mod_ref, b, d)
    h = _norm_mod(x, g_ref[...], scale, shift).astype(BF16)
    bcu = jnp.dot(h, win_ref[...], preferred_element_type=F32)
    b_gate, v = bcu[:, :d], bcu[:, d:2 * d] * bcu[:, 2 * d:]
    row = lax.broadcasted_iota(jnp.int32, (tm, 1), 0)
    prev = carry_ref[...]
    v1 = jnp.where(row == 0, prev[7:8], pltpu.roll(v, 1, 0))
    v2 = jnp.where(row == 0, prev[6:7], jnp.where(row == 1, prev[7:8], pltpu.roll(v, 2, 0)))
    cw = cw_ref[...]
    conv = cw[0:1] * v2 + cw[1:2] * v1 + cw[2:3] * v
    carry_ref[...] = v[tm - SUBLANES:, :]
    y = jnp.dot((b_gate * conv).astype(BF16), wout_ref[...], preferred_element_type=F32)
    o_ref[0] = x + gate * y


def _conv_mixer(x, mods, set_idx, g, w_in, conv_w, w_out, tm):
    bsz, s, d = x.shape
    return pl.pallas_call(
        _conv_mixer_kernel,
        out_shape=jax.ShapeDtypeStruct(x.shape, F32),
        grid=(bsz, s // tm),
        in_specs=[
            pl.BlockSpec((1, tm, d), lambda b, i: (b, i, 0)),
            pl.BlockSpec((1, SUBLANES, 3 * d), lambda b, i: (set_idx, 0, 0)),
            pl.BlockSpec((1, d), lambda b, i: (0, 0)),
            pl.BlockSpec((d, 3 * d), lambda b, i: (0, 0)),
            pl.BlockSpec((CONV_WIDTH, d), lambda b, i: (0, 0)),
            pl.BlockSpec((d, d), lambda b, i: (0, 0)),
        ],
        out_specs=pl.BlockSpec((1, tm, d), lambda b, i: (b, i, 0)),
        scratch_shapes=[pltpu.VMEM((SUBLANES, d), F32)],
        compiler_params=_cparams(("arbitrary", "arbitrary")),
        name="conv_mixer",
    )(x, mods, g, w_in, conv_w, w_out)


def _ffn_kernel(x_ref, mod_ref, g_ref, wg_ref, wu_ref, wd_ref, o_ref):
    b = pl.program_id(0)
    d = x_ref.shape[2]
    x = x_ref[0]
    shift, scale, gate = _split_mod(mod_ref, b, d)
    h = _norm_mod(x, g_ref[...], scale, shift).astype(BF16)
    a = _silu(jnp.dot(h, wg_ref[...], preferred_element_type=F32)) * jnp.dot(
        h, wu_ref[...], preferred_element_type=F32)
    y = jnp.dot(a.astype(BF16), wd_ref[...], preferred_element_type=F32)
    o_ref[0] = x + gate * y


def _dense_ffn(x, mods, set_idx, g, w_gate, w_up, w_down, tm):
    bsz, s, d = x.shape
    f = w_gate.shape[1]
    resident = pl.Buffered(1)
    return pl.pallas_call(
        _ffn_kernel,
        out_shape=jax.ShapeDtypeStruct(x.shape, F32),
        grid=(bsz, s // tm),
        in_specs=[
            pl.BlockSpec((1, tm, d), lambda b, i: (b, i, 0)),
            pl.BlockSpec((1, SUBLANES, 3 * d), lambda b, i: (set_idx, 0, 0)),
            pl.BlockSpec((1, d), lambda b, i: (0, 0)),
            pl.BlockSpec((d, f), lambda b, i: (0, 0), pipeline_mode=resident),
            pl.BlockSpec((d, f), lambda b, i: (0, 0), pipeline_mode=resident),
            pl.BlockSpec((f, d), lambda b, i: (0, 0), pipeline_mode=resident),
        ],
        out_specs=pl.BlockSpec((1, tm, d), lambda b, i: (b, i, 0)),
        compiler_params=_cparams(("parallel", "parallel")),
        name="dense_ffn",
    )(x, mods, g, w_gate, w_up, w_down)


def _class_major(ref, tm, r):
    n_chunks = ref.shape[0]
    if r == 1:
        return jnp.concatenate([ref[c] for c in range(n_chunks)], axis=1)
    return jnp.concatenate(
        [jnp.concatenate([ref[c, pl.ds(res, tm // r, stride=r), :] for c in range(n_chunks)], axis=1)
         for res in range(r)], axis=0)


def _store_lane_chunks(ref, val):
    for c in range(ref.shape[0]):
        ref[c] = val[:, c * LANES:(c + 1) * LANES]


def _rope(t, cos, sin_signed, first_half):
    out = []
    for c in range(t.shape[1] // LANES):
        tc = t[:, c * LANES:(c + 1) * LANES]
        rot = jnp.where(first_half, pltpu.roll(tc, LANES - HEAD_DIM // 2, 1),
                        pltpu.roll(tc, HEAD_DIM // 2, 1))
        out.append(tc * cos + rot * sin_signed)
    return jnp.concatenate(out, axis=1)


def _qkv_kernel(x_ref, mod_ref, g_ref, pos_ref, freq_ref, w_ref, o0_ref, o1_ref, o2_ref,
                h_scr, cos_scr, sin_scr):
    b = pl.program_id(0)
    tm, d = x_ref.shape[1], x_ref.shape[2]
    shift, scale, _ = _split_mod(mod_ref, b, d)
    _store_lane_chunks(h_scr, _norm_mod(x_ref[0], g_ref[...], scale, shift))
    ang = pos_ref[0].astype(F32) * freq_ref[...]
    first_half = (lax.broadcasted_iota(jnp.int32, (1, LANES), 1) % HEAD_DIM) < HEAD_DIM // 2
    cos_scr[0] = jnp.cos(ang)
    sin_scr[0] = jnp.where(first_half, -jnp.sin(ang), jnp.sin(ang))
    gl = GROUP_LANES
    for gi, ((_, r), o_ref) in enumerate(zip(ATTN_GROUPS, (o0_ref, o1_ref, o2_ref))):
        hg = _class_major(h_scr, tm, r).astype(BF16)
        cos, sin_signed = _class_major(cos_scr, tm, r), _class_major(sin_scr, tm, r)
        qkv = jnp.dot(hg, w_ref[gi], preferred_element_type=F32)
        q = _rope(qkv[:, :gl], cos, sin_signed, first_half) * (HEAD_DIM ** -0.5)
        k = _rope(qkv[:, gl:2 * gl], cos, sin_signed, first_half)
        o_ref[0, :, :gl] = q.astype(BF16)
        o_ref[0, :, gl:2 * gl] = k.astype(BF16)
        o_ref[0, :, 2 * gl:] = qkv[:, 2 * gl:].astype(BF16)


def _qkv_rope(x, mods, set_idx, g, pos3, freq, w_groups, tm):
    bsz, s, d = x.shape
    n_groups = len(ATTN_GROUPS)
    out = jax.ShapeDtypeStruct((bsz, s, 3 * GROUP_LANES), BF16)
    return pl.pallas_call(
        _qkv_kernel,
        out_shape=[out] * n_groups,
        grid=(bsz, s // tm),
        in_specs=[
            pl.BlockSpec((1, tm, d), lambda b, i: (b, i, 0)),
            pl.BlockSpec((1, SUBLANES, 3 * d), lambda b, i: (set_idx, 0, 0)),
            pl.BlockSpec((1, d), lambda b, i: (0, 0)),
            pl.BlockSpec((1, tm, 1), lambda b, i: (b, i, 0)),
            pl.BlockSpec((1, LANES), lambda b, i: (0, 0)),
            pl.BlockSpec((n_groups, d, 3 * GROUP_LANES), lambda b, i: (0, 0, 0)),
        ],
        out_specs=[pl.BlockSpec((1, tm, 3 * GROUP_LANES), lambda b, i: (b, i, 0))] * n_groups,
        scratch_shapes=[pltpu.VMEM((d // LANES, tm, LANES), F32), pltpu.VMEM((1, tm, LANES), F32),
                        pltpu.VMEM((1, tm, LANES), F32)],
        compiler_params=_cparams(("parallel", "parallel")),
        name="qkv_rope",
    )(x, mods, g, pos3, freq, w_groups)


def _store_rows(ref, a, val):
    st, per_tile = ATTN_STEPS, ref.shape[1]
    if per_tile >= st:
        ref[(a * st) // per_tile, pl.ds((a * st) % per_tile, st), :] = val
    else:
        for j in range(st // per_tile):
            ref[a * (st // per_tile) + j] = val[j * per_tile:(j + 1) * per_tile]


def _attn_kernel(q_ref, kc_ref, vc_ref, kp_ref, vp_ref, o_ref, lse_ref):
    i = pl.program_id(2)
    st = ATTN_STEPS
    tq = q_ref.shape[0] * q_ref.shape[1]
    q = q_ref[...].reshape(tq, GROUP_LANES)
    k_rows = jnp.concatenate([kp_ref[...].reshape(st, GROUP_LANES),
                              kc_ref[...].reshape(tq, GROUP_LANES)], axis=0)
    v_rows = jnp.concatenate([vp_ref[...].reshape(st, GROUP_LANES),
                              vc_ref[...].reshape(tq, GROUP_LANES)], axis=0)
    lane = lax.broadcasted_iota(jnp.int32, (1, LANES), 1)
    qi = lax.broadcasted_iota(jnp.int32, (st, 2 * st), 0)
    kj = lax.broadcasted_iota(jnp.int32, (st, 2 * st), 1)
    band = (kj >= qi) & (kj <= qi + st)
    band_first = band & ((kj >= st) | (i > 0))
    nt_dims = (((1,), (1,)), ((), ()))
    zero = jnp.zeros((), BF16)

    k_heads, v_heads = [], []
    for pair in range(PAIRS_PER_GROUP):
        sl = slice(pair * LANES, (pair + 1) * LANES)
        kp_, vp_ = k_rows[:, sl], v_rows[:, sl]
        if HEADS_PER_GROUP - 2 * pair >= 2:
            halves = [lane < HEAD_DIM, lane >= HEAD_DIM]
            k_heads.append([jnp.where(m, kp_, zero) for m in halves])
            v_heads.append([jnp.where(m, vp_, zero) for m in halves])
        else:
            k_heads.append([kp_])
            v_heads.append([vp_])

    blocks = [(a, pair, hh) for a in range(tq // st) for pair in range(PAIRS_PER_GROUP)
              for hh in range(len(k_heads[pair]))]
    scores = []
    for a, pair, hh in blocks:
        qa = q[a * st:(a + 1) * st, pair * LANES:(pair + 1) * LANES]
        s = lax.dot_general(qa, k_heads[pair][hh][a * st:(a + 2) * st], nt_dims,
                            preferred_element_type=F32)
        scores.append(jnp.where(band if a else band_first, s, NEG_INF))
    s_all = jnp.concatenate(scores, axis=0)
    m_all = jnp.max(s_all, axis=-1, keepdims=True)
    p_all = jnp.exp(s_all - m_all)
    l_all = jnp.sum(p_all, axis=-1, keepdims=True)
    p_all = p_all.astype(BF16)
    inv_all = 1.0 / l_all
    lse_all = m_all + jnp.log(l_all)

    for a in range(tq // st):
        lse_tile = jnp.zeros((st, LANES), F32)
        o_tiles = []
        for pair in range(PAIRS_PER_GROUP):
            acc = jnp.zeros((st, LANES), F32)
            for hh in range(len(k_heads[pair])):
                rows = slice(blocks.index((a, pair, hh)) * st, (blocks.index((a, pair, hh)) + 1) * st)
                pv = jnp.dot(p_all[rows], v_heads[pair][hh][a * st:(a + 2) * st],
                             preferred_element_type=F32)
                acc = acc + pv * inv_all[rows]
                lse_tile = jnp.where(lane == 2 * pair + hh, lse_all[rows], lse_tile)
            o_tiles.append(acc.astype(BF16))
        _store_rows(o_ref, a, jnp.concatenate(o_tiles, axis=1))
        _store_rows(lse_ref, a, lse_tile)


def _dilated_attn(qkv, r, tmp):
    bsz, s, width = qkv.shape
    gl, st = GROUP_LANES, ATTN_STEPS
    cls_rows = tmp // r
    tq = min(512, s // r)
    nt = tq // cls_rows
    n_tiles = s // tmp
    n_q = (s // r) // tq
    sub = tq // st
    main = qkv.reshape(bsz, n_tiles, r, cls_rows, width)

    def main_spec(col):
        return pl.BlockSpec((None, nt, None, cls_rows, gl), lambda b, res, i: (b, i, res, 0, col))

    if cls_rows >= st:
        per_tile = cls_rows // st
        prev_arr = qkv.reshape(bsz, n_tiles, r, per_tile, st, width)

        def prev_spec(col):
            def index(b, res, i):
                c = jnp.maximum(i * sub - 1, 0)
                return (b, c // per_tile, res, c % per_tile, 0, col)
            return pl.BlockSpec((None, None, None, None, st, gl), index)
    else:
        ntp = st // cls_rows
        prev_arr = qkv.reshape(bsz, n_tiles // ntp, ntp, r, cls_rows, width)

        def prev_spec(col):
            return pl.BlockSpec((None, None, ntp, None, cls_rows, gl),
                                lambda b, res, i: (b, jnp.maximum(i * sub - 1, 0), 0, res, 0, col))

    o5 = jax.ShapeDtypeStruct((bsz, n_tiles, r, cls_rows, gl), BF16)
    l5 = jax.ShapeDtypeStruct((bsz, n_tiles, r, cls_rows, LANES), F32)

    o, lse = pl.pallas_call(
        _attn_kernel,
        out_shape=[o5, l5],
        grid=(bsz, r, n_q),
        in_specs=[main_spec(0), main_spec(1), main_spec(2), prev_spec(1), prev_spec(2)],
        out_specs=[
            pl.BlockSpec((None, nt, None, cls_rows, gl), lambda b, res, i: (b, i, res, 0, 0)),
            pl.BlockSpec((None, nt, None, cls_rows, LANES), lambda b, res, i: (b, i, res, 0, 0)),
        ],
        compiler_params=_cparams(("parallel", "parallel", "parallel")),
        name=f"dilated_attn_r{r}",
    )(main, main, main, prev_arr, prev_arr)
    return o.reshape(bsz, s, gl), lse.reshape(bsz, s, LANES)


def _token_major(scr, val, tm, r):
    if r == 1:
        return val
    n = tm // r
    for c in range(scr.shape[0]):
        for res in range(r):
            scr[c, pl.ds(res, n, stride=r), :] = val[res * n:(res + 1) * n, c * LANES:(c + 1) * LANES]
    return jnp.concatenate([scr[c] for c in range(scr.shape[0])], axis=1)


def _attn_out_route_kernel(x_ref, mod_a_ref, mod_m_ref, g_ref, o0_ref, o1_ref, o2_ref,
                           l0_ref, l1_ref, l2_ref, expand_ref, wo_ref, wr_ref,
                           x_out_ref, h_out_ref, route_ref, wts_ref, cnt_ref,
                           o_scr, l_scr, base_scr):
    b, i = pl.program_id(0), pl.program_id(1)
    tm, d = x_ref.shape[1], x_ref.shape[2]
    n_exp = cnt_ref.shape[0]

    @pl.when((b == 0) & (i == 0))
    def _():
        base_scr[...] = jnp.zeros_like(base_scr)

    o_refs, l_refs = (o0_ref, o1_ref, o2_ref), (l0_ref, l1_ref, l2_ref)
    outs, lses = [], []
    for gi, (_, r) in enumerate(ATTN_GROUPS):
        lses.append(_token_major(l_scr.at[gi], l_refs[gi][0], tm, r))
        outs.append(_token_major(o_scr.at[gi], o_refs[gi][0].astype(F32), tm, r))
    m = jnp.maximum(jnp.maximum(lses[0], lses[1]), lses[2])
    es = [jnp.exp(l - m) for l in lses]
    inv = 1.0 / (es[0] + es[1] + es[2])
    scaled = []
    for gi in range(len(ATTN_GROUPS)):
        alpha = es[gi] * inv
        hi = alpha.astype(BF16)
        lo = (alpha - hi.astype(F32)).astype(BF16)
        wide = jnp.dot(hi, expand_ref[...], preferred_element_type=F32) + jnp.dot(
            lo, expand_ref[...], preferred_element_type=F32)
        scaled.append((outs[gi] * wide).astype(BF16))
    attn = jnp.dot(jnp.concatenate(scaled, axis=1), wo_ref[...], preferred_element_type=F32)
    _, _, gate_a = _split_mod(mod_a_ref, b, d)
    x = x_ref[0] + gate_a * attn
    x_out_ref[0] = x

    shift, scale, _ = _split_mod(mod_m_ref, b, d)
    h = _norm_mod(x, g_ref[...], scale, shift)
    for c in range(d // LANES):
        h_out_ref[pl.ds(c, tm, stride=d // LANES), :] = h[:, c * LANES:(c + 1) * LANES]

    lane = lax.broadcasted_iota(jnp.int32, (1, LANES), 1)
    h_hi = h.astype(BF16)
    h_lo = (h - h_hi.astype(F32)).astype(BF16)
    logits = (jnp.dot(h_hi, wr_ref[0], preferred_element_type=F32)
              + jnp.dot(h_lo, wr_ref[0], preferred_element_type=F32)
              + jnp.dot(h_hi, wr_ref[1], preferred_element_type=F32))
    logits = jnp.where(lane < n_exp, logits, NEG_INF)
    m1 = jnp.max(logits, axis=-1, keepdims=True)
    i1 = jnp.min(jnp.where(logits == m1, lane, LANES), axis=-1, keepdims=True)
    rest = jnp.where(lane == i1, NEG_INF, logits)
    m2 = jnp.max(rest, axis=-1, keepdims=True)
    i2 = jnp.min(jnp.where(rest == m2, lane, LANES), axis=-1, keepdims=True)
    e2 = jnp.exp(m2 - m1)
    w1 = 1.0 / (1.0 + e2)
    w2 = e2 * w1
    wts_ref[...] = jnp.where(lane == 0, w1, jnp.where(lane == 1, w2, 0.0))

    oh1, oh2 = (lane == i1), (lane == i2)
    picks = jnp.where(oh1 | oh2, 1.0, 0.0).astype(BF16)
    ri = lax.broadcasted_iota(jnp.int32, (tm, tm), 0)
    ci = lax.broadcasted_iota(jnp.int32, (tm, tm), 1)
    before = jnp.where(ci < ri, 1.0, 0.0).astype(BF16)
    rank = jnp.dot(before, picks, preferred_element_type=F32) + base_scr[0:1, :]
    r1 = jnp.sum(jnp.where(oh1, rank, 0.0), axis=-1, keepdims=True).astype(jnp.int32)
    r2 = jnp.sum(jnp.where(oh2, rank, 0.0), axis=-1, keepdims=True).astype(jnp.int32)
    route_ref[...] = jnp.where(lane == 0, r1, jnp.where(lane == 1, r2, jnp.where(
        lane == 2, i1, jnp.where(lane == 3, i2, 0))))
    total = base_scr[0:1, :] + jnp.sum(picks.astype(F32), axis=0, keepdims=True)
    base_scr[...] = jnp.broadcast_to(total, base_scr.shape)
    cnt_ref[...] = jnp.broadcast_to(total, cnt_ref.shape).astype(jnp.int32)


def _attn_out_route(x, mods, set_a, set_m, g, outs, lses, expand, w_o, w_router, tm):
    bsz, s, d = x.shape
    t = bsz * s
    n_i = s // tm
    n_groups = len(ATTN_GROUPS)
    gl = GROUP_LANES
    tok = lambda b, i: (b, i, 0)
    flat = lambda b, i: (b * n_i + i, 0)
    const2 = lambda b, i: (0, 0)
    return pl.pallas_call(
        _attn_out_route_kernel,
        out_shape=[
            jax.ShapeDtypeStruct((bsz, s, d), F32),
            jax.ShapeDtypeStruct((t * ROW_SLAB, LANES), F32),
            jax.ShapeDtypeStruct((t, LANES), jnp.int32),
            jax.ShapeDtypeStruct((t, LANES), F32),
            jax.ShapeDtypeStruct((SUBLANES, LANES), jnp.int32),
        ],
        grid=(bsz, n_i),
        in_specs=[
            pl.BlockSpec((1, tm, d), tok),
            pl.BlockSpec((1, SUBLANES, 3 * d), lambda b, i: (set_a, 0, 0)),
            pl.BlockSpec((1, SUBLANES, 3 * d), lambda b, i: (set_m, 0, 0)),
            pl.BlockSpec((1, d), const2),
        ] + [pl.BlockSpec((1, tm, gl), tok)] * n_groups
          + [pl.BlockSpec((1, tm, LANES), tok)] * n_groups + [
            pl.BlockSpec((LANES, gl), const2),
            pl.BlockSpec((n_groups * gl, d), const2),
            pl.BlockSpec((2, d, LANES), lambda b, i: (0, 0, 0)),
        ],
        out_specs=[
            pl.BlockSpec((1, tm, d), tok),
            pl.BlockSpec((tm * ROW_SLAB, LANES), flat),
            pl.BlockSpec((tm, LANES), flat),
            pl.BlockSpec((tm, LANES), flat),
            pl.BlockSpec((SUBLANES, LANES), const2),
        ],
        scratch_shapes=[pltpu.VMEM((n_groups, gl // LANES, tm, LANES), F32),
                        pltpu.VMEM((n_groups, 1, tm, LANES), F32),
                        pltpu.VMEM((SUBLANES, LANES), F32)],
        compiler_params=_cparams(("arbitrary", "arbitrary")),
        name="attn_out_route",
    )(x, mods, mods, g, *outs, *lses, expand, w_o, w_router)


DMA_LOOP_UNROLL = 16


def _for_each(n, fn):
    def body(g, carry):
        for u in range(DMA_LOOP_UNROLL):
            fn(g * DMA_LOOP_UNROLL + u)
        return carry
    lax.fori_loop(0, n // DMA_LOOP_UNROLL, body, 0)


def _row_copy(src, src_row, dst, dst_row, sem):
    return pltpu.make_async_copy(src.at[pl.ds(pl.multiple_of(src_row * ROW_SLAB, ROW_SLAB), ROW_SLAB)],
                                 dst.at[pl.ds(pl.multiple_of(dst_row * ROW_SLAB, ROW_SLAB), ROW_SLAB)],
                                 sem)


def _dispatch_kernel(fill_ref, pos_ref, h_ref, xs_hbm, zeros_scr, sem, zsem, *, tmd, te, n_exp,
                     n_tiles):
    step = pl.program_id(0)

    def zero_fills():
        def zero_copy(first_row, n):
            start = pl.multiple_of(first_row * ROW_SLAB, ROW_SLAB)
            return pltpu.make_async_copy(zeros_scr.at[pl.ds(0, n * ROW_SLAB)],
                                         xs_hbm.at[pl.ds(start, n * ROW_SLAB)], zsem)
        fills = []
        for e in range(n_exp):
            row, length = fill_ref[e], fill_ref[n_exp + e]
            for bit in reversed(range(te.bit_length() - 1)):
                take = (length & (1 << bit)) != 0
                fills.append((take, zero_copy(row, 1 << bit)))
                row = row + jnp.where(take, 1 << bit, 0)
        n_used = fill_ref[2 * n_exp]
        for k in range(n_exp):
            fills.append((n_used + k < n_tiles, zero_copy((n_used + k) * te, te)))
        return fills

    @pl.when(step == 0)
    def _():
        zeros_scr[...] = jnp.zeros_like(zeros_scr)
        for take, cp in zero_fills():
            pl.when(take)(cp.start)
        for take, cp in zero_fills():
            pl.when(take)(cp.wait)

    def copies(tok):
        return [_row_copy(h_ref, tok, xs_hbm, pos_ref[0, 0, tok * TOP_K + k], sem)
                for k in range(TOP_K)]

    _for_each(tmd, lambda tok: [cp.start(priority=k) for k, cp in enumerate(copies(tok))])
    _for_each(tmd, lambda tok: [cp.wait() for cp in copies(tok)])


def _moe_dispatch(fill, pos_blocks, h_rows, n_tiles, n_exp, tmd, te):
    n_steps = pos_blocks.shape[0]
    assert te & (te - 1) == 0
    kernel = functools.partial(_dispatch_kernel, tmd=tmd, te=te, n_exp=n_exp, n_tiles=n_tiles)
    return pl.pallas_call(
        kernel,
        out_shape=jax.ShapeDtypeStruct((n_tiles * te * ROW_SLAB, LANES), F32),
        grid_spec=pltpu.PrefetchScalarGridSpec(
            num_scalar_prefetch=1,
            grid=(n_steps,),
            in_specs=[
                pl.BlockSpec((1, 1, tmd * TOP_K), lambda s, cnt: (s, 0, 0), memory_space=pltpu.SMEM),
                pl.BlockSpec((tmd * ROW_SLAB, LANES), lambda s, cnt: (s, 0)),
            ],
            out_specs=pl.BlockSpec(memory_space=pl.ANY),
            scratch_shapes=[pltpu.VMEM((te * ROW_SLAB, LANES), F32), pltpu.SemaphoreType.DMA,
                            pltpu.SemaphoreType.DMA],
        ),
        compiler_params=_cparams(("arbitrary",)),
        name="moe_dispatch",
    )(fill, pos_blocks, h_rows)


def _slab_rows_to_matrix(ref, n_rows, n_chunks):
    return jnp.concatenate([ref[pl.ds(c, n_rows, stride=n_chunks), :] for c in range(n_chunks)],
                           axis=1)


def _experts_kernel(tile_expert_ref, n_used_ref, x_ref, wg_ref, wu_ref, wd_ref, y_ref, *, te,
                    f_chunks):
    del tile_expert_ref
    used = pl.program_id(0) < n_used_ref[0]

    @pl.when(jnp.logical_not(used))
    def _():
        y_ref[...] = jnp.zeros_like(y_ref)

    @pl.when(used)
    def _():
        d = wg_ref.shape[1]
        n_chunks = d // LANES
        x = _slab_rows_to_matrix(x_ref, te, n_chunks).astype(BF16)
        y = jnp.zeros((te, d), F32)
        for lo, hi in f_chunks:
            gate = jnp.dot(x, wg_ref[0, :, lo:hi], preferred_element_type=F32)
            up = jnp.dot(x, wu_ref[0, :, lo:hi], preferred_element_type=F32)
            y = y + jnp.dot((_silu(gate) * up).astype(BF16), wd_ref[0, lo:hi, :],
                            preferred_element_type=F32)
        for c in range(n_chunks):
            y_ref[pl.ds(c, te, stride=n_chunks), :] = y[:, c * LANES:(c + 1) * LANES]


def _f_chunks(f):
    if f <= 1536:
        return ((0, f),)
    half = (f // 2 + 255) // 256 * 256
    return ((0, half), (half, f))


def _moe_experts(tile_expert, n_used, xs, w_gate, w_up, w_down, te):
    n_tiles = tile_expert.shape[0]
    _, d, f = w_gate.shape
    kernel = functools.partial(_experts_kernel, te=te, f_chunks=_f_chunks(f))
    weights = lambda i, te_, nu_: (te_[i], 0, 0)
    return pl.pallas_call(
        kernel,
        out_shape=jax.ShapeDtypeStruct(xs.shape, F32),
        grid_spec=pltpu.PrefetchScalarGridSpec(
            num_scalar_prefetch=2,
            grid=(n_tiles,),
            in_specs=[
                pl.BlockSpec((te * ROW_SLAB, LANES), lambda i, te_, nu_: (jnp.minimum(i, nu_[0] - 1), 0)),
                pl.BlockSpec((1, d, f), weights),
                pl.BlockSpec((1, d, f), weights),
                pl.BlockSpec((1, f, d), weights),
            ],
            out_specs=pl.BlockSpec((te * ROW_SLAB, LANES), lambda i, te_, nu_: (i, 0)),
        ),
        compiler_params=_cparams(("arbitrary",)),
        name="moe_experts",
    )(tile_expert, n_used, xs, w_gate, w_up, w_down)


def _combine_kernel(pos_ref, pos_next_ref, x_ref, mod_ref, g_ref, wts_ref, ys_hbm, o_ref, buf, sems,
                    *, tmc):
    b = pl.program_id(0)
    step = b * pl.num_programs(1) + pl.program_id(1)
    n_steps = pl.num_programs(0) * pl.num_programs(1)
    slot = step % 2
    d = x_ref.shape[2]
    n_chunks = d // LANES

    def copies(p_ref, slot_, tok):
        return [_row_copy(ys_hbm, p_ref[0, 0, tok * TOP_K + k], buf.at[slot_], k * tmc + tok,
                          sems.at[slot_]) for k in range(TOP_K)]

    def fetch(p_ref, slot_):
        _for_each(tmc, lambda tok: [cp.start(priority=k)
                                    for k, cp in enumerate(copies(p_ref, slot_, tok))])

    pl.when(step == 0)(lambda: fetch(pos_ref, slot))
    pl.when(step + 1 < n_steps)(lambda: fetch(pos_next_ref, 1 - slot))
    _for_each(tmc, lambda tok: [cp.wait() for cp in copies(pos_ref, slot, tok)])

    wts = wts_ref[...]
    y = jnp.zeros((tmc, d), F32)
    rows = buf.at[slot]
    for k in range(TOP_K):
        yk = _slab_rows_to_matrix(rows.at[pl.ds(k * tmc * ROW_SLAB, tmc * ROW_SLAB)], tmc, n_chunks)
        y = y + wts[:, k:k + 1] * yk
    _, _, gate = _split_mod(mod_ref, b, d)
    x = x_ref[0] + gate * y
    o_ref[0] = x * lax.rsqrt(jnp.mean(x * x, axis=-1, keepdims=True) + RMS_EPS) * g_ref[...]


def _moe_combine(pos_blocks, x, mods, set_idx, final_g, wts, ys, tmc):
    bsz, s, d = x.shape
    n_i = s // tmc
    last = bsz * n_i - 1
    kernel = functools.partial(_combine_kernel, tmc=tmc)
    return pl.pallas_call(
        kernel,
        out_shape=jax.ShapeDtypeStruct(x.shape, F32),
        grid=(bsz, n_i),
        in_specs=[
            pl.BlockSpec((1, 1, tmc * TOP_K), lambda b, i: (b * n_i + i, 0, 0),
                         memory_space=pltpu.SMEM),
            pl.BlockSpec((1, 1, tmc * TOP_K), lambda b, i: (jnp.minimum(b * n_i + i + 1, last), 0, 0),
                         memory_space=pltpu.SMEM),
            pl.BlockSpec((1, tmc, d), lambda b, i: (b, i, 0)),
            pl.BlockSpec((1, SUBLANES, 3 * d), lambda b, i: (set_idx, 0, 0)),
            pl.BlockSpec((1, d), lambda b, i: (0, 0)),
            pl.BlockSpec((tmc, LANES), lambda b, i: (b * n_i + i, 0)),
            pl.BlockSpec(memory_space=pl.ANY),
        ],
        out_specs=pl.BlockSpec((1, tmc, d), lambda b, i: (b, i, 0)),
        scratch_shapes=[pltpu.VMEM((2, TOP_K * tmc * ROW_SLAB, LANES), F32),
                        pltpu.SemaphoreType.DMA((2,))],
        compiler_params=_cparams(("arbitrary", "arbitrary")),
        name="moe_combine",
    )(pos_blocks, pos_blocks, x, mods, final_g, wts, ys)


def _pad_group_columns(w, which):
    width = HEADS_PER_GROUP * HEAD_DIM
    n_groups = len(ATTN_GROUPS)
    cols = w[:, which * n_groups * width:(which + 1) * n_groups * width]
    cols = cols.reshape(w.shape[0], n_groups, width)
    return jnp.pad(cols, ((0, 0), (0, 0), (0, GROUP_LANES - width)))


def kernel(x, c, positions, mod_w, mod_b, norm_g, conv_w_in, conv_w, conv_w_out, ffn_w_gate,
           ffn_w_up, ffn_w_down, attn_w_qkv, attn_w_o, router_w, moe_w_gate, moe_w_up, moe_w_down,
           final_g):
    bsz, s, d = x.shape
    t = bsz * s
    n_groups = len(ATTN_GROUPS)
    n_exp = router_w.shape[-1]
    assert all(w // r == ATTN_STEPS for w, r in ATTN_GROUPS)
    assert bsz <= SUBLANES and n_exp <= SUBLANES and d == ROW_SLAB * LANES
    tm = min(512, s)
    assert s % tm == 0 and all(s % (r * ATTN_STEPS) == 0 and tm % r == 0 for _, r in ATTN_GROUPS)

    c_pad = jnp.pad(c.astype(F32), ((0, SUBLANES - bsz), (0, 0)))
    mods = _ada_params(c_pad, mod_w.reshape(-1, d, 3 * d), mod_b.reshape(-1, 1, 3 * d))

    x = _conv_mixer(x, mods, 0, norm_g[0, 0][None], conv_w_in[0].astype(BF16), conv_w[0],
                    conv_w_out[0].astype(BF16), tm)
    x = _dense_ffn(x, mods, 1, norm_g[0, 1][None], ffn_w_gate[0].astype(BF16),
                   ffn_w_up[0].astype(BF16), ffn_w_down[0].astype(BF16), tm)

    w_qkv = attn_w_qkv[0]
    w_groups = jnp.concatenate([_pad_group_columns(w_qkv, which) for which in range(3)], axis=2)
    w_groups = w_groups.transpose(1, 0, 2).astype(BF16)
    inv_freq = ROPE_THETA ** (-jnp.arange(0, HEAD_DIM, 2, dtype=F32) / HEAD_DIM)
    freq = jnp.tile(inv_freq, LANES // (HEAD_DIM // 2))[None]
    qkvs = _qkv_rope(x, mods, 2, norm_g[1, 0][None], positions[..., None], freq, w_groups, tm)
    outs, lses = zip(*[_dilated_attn(qkv, r, tm) for qkv, (_, r) in zip(qkvs, ATTN_GROUPS)])

    width = HEADS_PER_GROUP * HEAD_DIM
    w_o = jnp.pad(attn_w_o[0].reshape(n_groups, width, d), ((0, 0), (0, GROUP_LANES - width), (0, 0)))
    w_o = w_o.reshape(n_groups * GROUP_LANES, d).astype(BF16)
    expand = (jnp.arange(GROUP_LANES)[None, :] // HEAD_DIM == jnp.arange(LANES)[:, None]).astype(BF16)
    w_router = jnp.pad(router_w[0], ((0, 0), (0, LANES - n_exp)))
    w_router_hi = w_router.astype(BF16)
    w_router = jnp.stack([w_router_hi, (w_router - w_router_hi.astype(F32)).astype(BF16)])

    x, h_rows, route, wts, counts = _attn_out_route(
        x, mods, 2, 3, norm_g[1, 1][None], outs, lses, expand, w_o, w_router, tm)

    te = min(512, t)
    n_tiles = (t * TOP_K) // te + n_exp
    counts = counts[0, :n_exp]
    tiles_per_expert = (counts + te - 1) // te
    ends = jnp.cumsum(tiles_per_expert)
    first_row = (ends - tiles_per_expert) * te
    n_used = ends[-1:]
    pos_flat = (first_row[route[:, TOP_K:2 * TOP_K]] + route[:, :TOP_K]).reshape(-1)
    fill = jnp.concatenate([first_row + counts, tiles_per_expert * te - counts, n_used]).astype(jnp.int32)
    idx = jnp.minimum(jnp.arange(n_tiles, dtype=jnp.int32), n_used - 1)
    tile_expert = jnp.sum(idx[:, None] >= ends[None, :], axis=1).astype(jnp.int32)

    pos_blocks = pos_flat.reshape(t // tm, 1, tm * TOP_K)
    xs = _moe_dispatch(fill, pos_blocks, h_rows, n_tiles, n_exp, tm, te)
    ys = _moe_experts(tile_expert, n_used.astype(jnp.int32), xs, moe_w_gate[0].astype(BF16),
                      moe_w_up[0].astype(BF16), moe_w_down[0].astype(BF16), te)
    return _moe_combine(pos_blocks, x, mods, 3, final_g[None], wts, ys, tm)
```

```python
import functools
import math

import jax
import jax.numpy as jnp
from jax import lax
from jax.experimental import pallas as pl
from jax.experimental.pallas import tpu as pltpu

CONV_WIDTH = 3
ATTN_GROUPS = ((128, 1), (512, 4), (2048, 16))
HEADS_PER_GROUP = 5
HEAD_DIM = 64
ROPE_THETA = 10000.0
TOP_K = 2
RMS_EPS = 1e-6

LANES = 128
SUBLANES = 8
VMEM_LIMIT_BYTES = 56 * 1024 * 1024

GROUP_LANES = 384
PAIRS_PER_GROUP = GROUP_LANES // LANES
ATTN_STEPS = 128
ROW_SLAB = 8
ROUTE_ROW_CHUNKS = 2

F32 = jnp.float32
BF16 = jnp.bfloat16
NEG_INF = float("-inf")


def _cparams(semantics):
    return pltpu.CompilerParams(dimension_semantics=semantics, vmem_limit_bytes=VMEM_LIMIT_BYTES)


def _norm_mod(x, g, scale, shift):
    y = x * lax.rsqrt(jnp.mean(x * x, axis=-1, keepdims=True) + RMS_EPS)
    return (y * g) * (1.0 + scale) + shift


def _split_mod(mod_ref, b, d):
    row = mod_ref[0, pl.ds(b, 1), :]
    return row[:, :d], row[:, d:2 * d], row[:, 2 * d:]


def _silu(x):
    return x * jax.nn.sigmoid(x)


def _ada_kernel(c_ref, w_ref, b_ref, o_ref):
    s = _silu(c_ref[...])
    o_ref[0] = jnp.dot(s, w_ref[0], precision=lax.Precision.HIGHEST,
                       preferred_element_type=F32) + b_ref[0]


def _ada_params(c_pad, mod_w, mod_b):
    n_sets, d, d3 = mod_w.shape
    tn = d
    return pl.pallas_call(
        _ada_kernel,
        out_shape=jax.ShapeDtypeStruct((n_sets, SUBLANES, d3), F32),
        grid=(n_sets, d3 // tn),
        in_specs=[
            pl.BlockSpec((SUBLANES, d), lambda s, j: (0, 0)),
            pl.BlockSpec((1, d, tn), lambda s, j: (s, 0, j)),
            pl.BlockSpec((1, 1, tn), lambda s, j: (s, 0, j)),
        ],
        out_specs=pl.BlockSpec((1, SUBLANES, tn), lambda s, j: (s, 0, j)),
        compiler_params=_cparams(("parallel", "parallel")),
        name="ada_params",
    )(c_pad, mod_w, mod_b)


def _conv_mixer_kernel(x_ref, mod_ref, g_ref, win_ref, cw_ref, wout_ref, o_ref, carry_ref):
    b, i = pl.program_id(0), pl.program_id(1)
    tm, d = x_ref.shape[1], x_ref.shape[2]

    @pl.when(i == 0)
    def _():
        carry_ref[...] = jnp.zeros_like(carry_ref)

    x = x_ref[0]
    shift, scale, gate = _split_mod(mod_ref, b, d)
    h = _norm_mod(x, g_ref[...], scale, shift).astype(BF16)
    bcu = jnp.dot(h, win_ref[...], preferred_element_type=F32)
    b_gate, v = bcu[:, :d], bcu[:, d:2 * d] * bcu[:, 2 * d:]
    row = lax.broadcasted_iota(jnp.int32, (tm, 1), 0)
    prev = carry_ref[...]
    v1 = jnp.where(row == 0, prev[7:8], pltpu.roll(v, 1, 0))
    v2 = jnp.where(row == 0, prev[6:7], jnp.where(row == 1, prev[7:8], pltpu.roll(v, 2, 0)))
    cw = cw_ref[...]
    conv = cw[0:1] * v2 + cw[1:2] * v1 + cw[2:3] * v
    carry_ref[...] = v[tm - SUBLANES:, :]
    y = jnp.dot((b_gate * conv).astype(BF16), wout_ref[...], preferred_element_type=F32)
    o_ref[0] = x + gate * y


def _conv_mixer(x, mods, set_idx, g, w_in, conv_w, w_out, tm):
    bsz, s, d = x.shape
    return pl.pallas_call(
        _conv_mixer_kernel,
        out_shape=jax.ShapeDtypeStruct(x.shape, F32),
        grid=(bsz, s // tm),
        in_specs=[
            pl.BlockSpec((1, tm, d), lambda b, i: (b, i, 0)),
            pl.BlockSpec((1, SUBLANES, 3 * d), lambda b, i: (set_idx, 0, 0)),
            pl.BlockSpec((1, d), lambda b, i: (0, 0)),
            pl.BlockSpec((d, 3 * d), lambda b, i: (0, 0)),
            pl.BlockSpec((CONV_WIDTH, d), lambda b, i: (0, 0)),
            pl.BlockSpec((d, d), lambda b, i: (0, 0)),
        ],
        out_specs=pl.BlockSpec((1, tm, d), lambda b, i: (b, i, 0)),
        scratch_shapes=[pltpu.VMEM((SUBLANES, d), F32)],
        compiler_params=_cparams(("arbitrary", "arbitrary")),
        name="conv_mixer",
    )(x, mods, g, w_in, conv_w, w_out)


def _ffn_kernel(x_ref, mod_ref, g_ref, wg_ref, wu_ref, wd_ref, o_ref):
    b = pl.program_id(0)
    d = x_ref.shape[2]
    x = x_ref[0]
    shift, scale, gate = _split_mod(mod_ref, b, d)
    h = _norm_mod(x, g_ref[...], scale, shift).astype(BF16)
    a = _silu(jnp.dot(h, wg_ref[...], preferred_element_type=F32)) * jnp.dot(
        h, wu_ref[...], preferred_element_type=F32)
    y = jnp.dot(a.astype(BF16), wd_ref[...], preferred_element_type=F32)
    o_ref[0] = x + gate * y


def _dense_ffn(x, mods, set_idx, g, w_gate, w_up, w_down, tm):
    bsz, s, d = x.shape
    f = w_gate.shape[1]
    resident = pl.Buffered(1)
    return pl.pallas_call(
        _ffn_kernel,
        out_shape=jax.ShapeDtypeStruct(x.shape, F32),
        grid=(bsz, s // tm),
        in_specs=[
            pl.BlockSpec((1, tm, d), lambda b, i: (b, i, 0)),
            pl.BlockSpec((1, SUBLANES, 3 * d), lambda b, i: (set_idx, 0, 0)),
            pl.BlockSpec((1, d), lambda b, i: (0, 0)),
            pl.BlockSpec((d, f), lambda b, i: (0, 0), pipeline_mode=resident),
            pl.BlockSpec((d, f), lambda b, i: (0, 0), pipeline_mode=resident),
            pl.BlockSpec((f, d), lambda b, i: (0, 0), pipeline_mode=resident),
        ],
        out_specs=pl.BlockSpec((1, tm, d), lambda b, i: (b, i, 0)),
        compiler_params=_cparams(("parallel", "parallel")),
        name="dense_ffn",
    )(x, mods, g, w_gate, w_up, w_down)


def _class_major(ref, tm, r_from, r_to):
    n_chunks = ref.shape[0]
    if r_to == r_from:
        return jnp.concatenate([ref[c] for c in range(n_chunks)], axis=1)

    def rows(res):
        start = (res % r_from) * (tm // r_from) + res // r_from
        return pl.ds(start, tm // r_to, stride=r_to // r_from)

    return jnp.concatenate(
        [jnp.concatenate([ref[c, rows(res), :] for c in range(n_chunks)], axis=1)
         for res in range(r_to)], axis=0)


def _split_matmul(a, b01):
    hi = a.astype(BF16)
    lo = (a - hi.astype(F32)).astype(BF16)
    return jnp.dot(hi, b01, preferred_element_type=F32) + jnp.dot(lo, b01, preferred_element_type=F32)


def _rotary_tables(pos, freq, tm):
    half = HEAD_DIM // 2
    per_row = LANES // half
    q4 = tm // per_row
    lane = lax.broadcasted_iota(jnp.int32, (1, LANES), 1)
    pos4 = pos[(per_row - 1) * q4:]
    for j in reversed(range(per_row - 1)):
        pos4 = jnp.where(lane // half == j, pos[j * q4:(j + 1) * q4], pos4)
    ang = pos4 * freq
    cos4, sin4 = jnp.cos(ang), jnp.sin(ang)
    li = lax.broadcasted_iota(jnp.int32, (LANES, LANES), 0)
    lj = lax.broadcasted_iota(jnp.int32, (LANES, LANES), 1)
    same_freq = li % half == lj % half
    sign = jnp.where(lj % HEAD_DIM < half, -1.0, 1.0)
    cos_rows, sin_rows = [], []
    for j in range(per_row):
        pick = same_freq & (li // half == j)
        cos_rows.append(_split_matmul(cos4, jnp.where(pick, 1.0, 0.0).astype(BF16)))
        sin_rows.append(_split_matmul(sin4, jnp.where(pick, sign, 0.0).astype(BF16)))
    return jnp.concatenate(cos_rows, axis=0), jnp.concatenate(sin_rows, axis=0)


def _store_lane_chunks(ref, val):
    for c in range(ref.shape[0]):
        ref[c] = val[:, c * LANES:(c + 1) * LANES]


def _rope(t, cos, sin_signed, first_half):
    out = []
    for c in range(t.shape[1] // LANES):
        tc = t[:, c * LANES:(c + 1) * LANES]
        rot = jnp.where(first_half, pltpu.roll(tc, LANES - HEAD_DIM // 2, 1),
                        pltpu.roll(tc, HEAD_DIM // 2, 1))
        out.append(tc * cos + rot * sin_signed)
    return jnp.concatenate(out, axis=1)


def _qkv_kernel(x_ref, mod_ref, g_ref, pos_ref, freq_ref, w_ref, o0_ref, o1_ref, o2_ref,
                scr_a, scr_b):
    b = pl.program_id(0)
    tm, d = x_ref.shape[1], x_ref.shape[2]
    shift, scale, _ = _split_mod(mod_ref, b, d)
    first_half = (lax.broadcasted_iota(jnp.int32, (1, LANES), 1) % HEAD_DIM) < HEAD_DIM // 2
    cos_t, sin_t = _rotary_tables(pos_ref[0].astype(F32), freq_ref[...], tm)
    _store_lane_chunks(scr_a, jnp.concatenate(
        [_norm_mod(x_ref[0], g_ref[...], scale, shift), cos_t, sin_t], axis=1))
    scrs, cur, cur_r = (scr_a, scr_b), 0, 1
    gl = GROUP_LANES
    for gi, ((_, r), o_ref) in enumerate(zip(ATTN_GROUPS, (o0_ref, o1_ref, o2_ref))):
        vals = _class_major(scrs[cur], tm, cur_r, r)
        if r != cur_r and gi + 1 < len(ATTN_GROUPS):
            cur, cur_r = 1 - cur, r
            _store_lane_chunks(scrs[cur], vals)
        hg, cos, sin_signed = vals[:, :d].astype(BF16), vals[:, d:d + LANES], vals[:, d + LANES:]
        qkv = jnp.dot(hg, w_ref[gi], preferred_element_type=F32)
        q = _rope(qkv[:, :gl], cos, sin_signed, first_half) * (HEAD_DIM ** -0.5)
        k = _rope(qkv[:, gl:2 * gl], cos, sin_signed, first_half)
        o_ref[0, :, :gl] = q.astype(BF16)
        o_ref[0, :, gl:2 * gl] = k.astype(BF16)
        o_ref[0, :, 2 * gl:] = qkv[:, 2 * gl:].astype(BF16)


def _qkv_rope(x, mods, set_idx, g, pos3, freq, w_groups, tm):
    bsz, s, d = x.shape
    n_groups = len(ATTN_GROUPS)
    out = jax.ShapeDtypeStruct((bsz, s, 3 * GROUP_LANES), BF16)
    return pl.pallas_call(
        _qkv_kernel,
        out_shape=[out] * n_groups,
        grid=(bsz, s // tm),
        in_specs=[
            pl.BlockSpec((1, tm, d), lambda b, i: (b, i, 0)),
            pl.BlockSpec((1, SUBLANES, 3 * d), lambda b, i: (set_idx, 0, 0)),
            pl.BlockSpec((1, d), lambda b, i: (0, 0)),
            pl.BlockSpec((1, tm, 1), lambda b, i: (b, i, 0)),
            pl.BlockSpec((1, LANES), lambda b, i: (0, 0)),
            pl.BlockSpec((n_groups, d, 3 * GROUP_LANES), lambda b, i: (0, 0, 0)),
        ],
        out_specs=[pl.BlockSpec((1, tm, 3 * GROUP_LANES), lambda b, i: (b, i, 0))] * n_groups,
        scratch_shapes=[pltpu.VMEM((d // LANES + 2, tm, LANES), F32)] * 2,
        compiler_params=_cparams(("parallel", "parallel")),
        name="qkv_rope",
    )(x, mods, g, pos3, freq, w_groups)


def _store_rows(ref, a, val):
    st, per_tile = ATTN_STEPS, ref.shape[1]
    if per_tile >= st:
        ref[(a * st) // per_tile, pl.ds((a * st) % per_tile, st), :] = val
    else:
        for j in range(st // per_tile):
            ref[a * (st // per_tile) + j] = val[j * per_tile:(j + 1) * per_tile]


def _attn_kernel(q_ref, kc_ref, vc_ref, kp_ref, vp_ref, o_ref, lse_ref):
    i = pl.program_id(2)
    st = ATTN_STEPS
    tq = q_ref.shape[0] * q_ref.shape[1]
    q = q_ref[...].reshape(tq, GROUP_LANES)
    k_rows = jnp.concatenate([kp_ref[...].reshape(st, GROUP_LANES),
                              kc_ref[...].reshape(tq, GROUP_LANES)], axis=0)
    v_rows = jnp.concatenate([vp_ref[...].reshape(st, GROUP_LANES),
                              vc_ref[...].reshape(tq, GROUP_LANES)], axis=0)
    lane = lax.broadcasted_iota(jnp.int32, (1, LANES), 1)
    qi = lax.broadcasted_iota(jnp.int32, (st, 2 * st), 0)
    kj = lax.broadcasted_iota(jnp.int32, (st, 2 * st), 1)
    band = (kj >= qi) & (kj <= qi + st)
    band_first = band & ((kj >= st) | (i > 0))
    nt_dims = (((1,), (1,)), ((), ()))
    zero = jnp.zeros((), BF16)

    k_heads, v_heads = [], []
    for pair in range(PAIRS_PER_GROUP):
        sl = slice(pair * LANES, (pair + 1) * LANES)
        kp_, vp_ = k_rows[:, sl], v_rows[:, sl]
        if HEADS_PER_GROUP - 2 * pair >= 2:
            halves = [lane < HEAD_DIM, lane >= HEAD_DIM]
            k_heads.append([jnp.where(m, kp_, zero) for m in halves])
            v_heads.append([jnp.where(m, vp_, zero) for m in halves])
        else:
            k_heads.append([kp_])
            v_heads.append([vp_])

    blocks = [(a, pair, hh) for a in range(tq // st) for pair in range(PAIRS_PER_GROUP)
              for hh in range(len(k_heads[pair]))]
    scores = []
    for a, pair, hh in blocks:
        qa = q[a * st:(a + 1) * st, pair * LANES:(pair + 1) * LANES]
        s = lax.dot_general(qa, k_heads[pair][hh][a * st:(a + 2) * st], nt_dims,
                            preferred_element_type=F32)
        scores.append(jnp.where(band if a else band_first, s, NEG_INF))
    s_all = jnp.concatenate(scores, axis=0)
    m_all = jnp.max(s_all, axis=-1, keepdims=True)
    p_all = jnp.exp(s_all - m_all)
    l_all = jnp.sum(p_all, axis=-1, keepdims=True)
    p_all = p_all.astype(BF16)
    inv_all = 1.0 / l_all
    lse_all = m_all + jnp.log(l_all)

    for a in range(tq // st):
        lse_tile = jnp.zeros((st, LANES), F32)
        o_tiles = []
        for pair in range(PAIRS_PER_GROUP):
            acc = jnp.zeros((st, LANES), F32)
            for hh in range(len(k_heads[pair])):
                rows = slice(blocks.index((a, pair, hh)) * st, (blocks.index((a, pair, hh)) + 1) * st)
                pv = jnp.dot(p_all[rows], v_heads[pair][hh][a * st:(a + 2) * st],
                             preferred_element_type=F32)
                acc = acc + pv * inv_all[rows]
                lse_tile = jnp.where(lane == 2 * pair + hh, lse_all[rows], lse_tile)
            o_tiles.append(acc.astype(BF16))
        _store_rows(o_ref, a, jnp.concatenate(o_tiles, axis=1))
        _store_rows(lse_ref, a, lse_tile)


def _dilated_attn(qkv, r, tmp):
    bsz, s, width = qkv.shape
    gl, st = GROUP_LANES, ATTN_STEPS
    cls_rows = tmp // r
    tq = min(512, s // r)
    nt = tq // cls_rows
    n_tiles = s // tmp
    n_q = (s // r) // tq
    sub = tq // st
    main = qkv.reshape(bsz, n_tiles, r, cls_rows, width)

    def main_spec(col):
        return pl.BlockSpec((None, nt, None, cls_rows, gl), lambda b, res, i: (b, i, res, 0, col))

    if cls_rows >= st:
        per_tile = cls_rows // st
        prev_arr = qkv.reshape(bsz, n_tiles, r, per_tile, st, width)

        def prev_spec(col):
            def index(b, res, i):
                c = jnp.maximum(i * sub - 1, 0)
                return (b, c // per_tile, res, c % per_tile, 0, col)
            return pl.BlockSpec((None, None, None, None, st, gl), index)
    else:
        ntp = st // cls_rows
        prev_arr = qkv.reshape(bsz, n_tiles // ntp, ntp, r, cls_rows, width)

        def prev_spec(col):
            return pl.BlockSpec((None, None, ntp, None, cls_rows, gl),
                                lambda b, res, i: (b, jnp.maximum(i * sub - 1, 0), 0, res, 0, col))

    o5 = jax.ShapeDtypeStruct((bsz, n_tiles, r, cls_rows, gl), BF16)
    l5 = jax.ShapeDtypeStruct((bsz, n_tiles, r, cls_rows, LANES), F32)

    o, lse = pl.pallas_call(
        _attn_kernel,
        out_shape=[o5, l5],
        grid=(bsz, r, n_q),
        in_specs=[main_spec(0), main_spec(1), main_spec(2), prev_spec(1), prev_spec(2)],
        out_specs=[
            pl.BlockSpec((None, nt, None, cls_rows, gl), lambda b, res, i: (b, i, res, 0, 0)),
            pl.BlockSpec((None, nt, None, cls_rows, LANES), lambda b, res, i: (b, i, res, 0, 0)),
        ],
        compiler_params=_cparams(("parallel", "parallel", "parallel")),
        name=f"dilated_attn_r{r}",
    )(main, main, main, prev_arr, prev_arr)
    return o.reshape(bsz, s, gl), lse.reshape(bsz, s, LANES)


def _token_major(scr, val, tm, r):
    if r == 1:
        return val
    n = tm // r
    for c in range(scr.shape[0]):
        for res in range(r):
            scr[c, pl.ds(res, n, stride=r), :] = val[res * n:(res + 1) * n, c * LANES:(c + 1) * LANES]
    return jnp.concatenate([scr[c] for c in range(scr.shape[0])], axis=1)


def _attn_out_route_kernel(x_ref, mod_a_ref, mod_m_ref, g_ref, o0_ref, o1_ref, o2_ref,
                           l0_ref, l1_ref, l2_ref, expand_ref, wo_ref, wr_ref,
                           x_out_ref, h_out_ref, route_ref, wts_ref, cnt_ref,
                           o_scr, l_scr, base_scr):
    b, i = pl.program_id(0), pl.program_id(1)
    tm, d = x_ref.shape[1], x_ref.shape[2]
    n_exp = cnt_ref.shape[0]

    @pl.when((b == 0) & (i == 0))
    def _():
        base_scr[...] = jnp.zeros_like(base_scr)

    o_refs, l_refs = (o0_ref, o1_ref, o2_ref), (l0_ref, l1_ref, l2_ref)
    outs, lses = [], []
    for gi, (_, r) in enumerate(ATTN_GROUPS):
        lses.append(_token_major(l_scr.at[gi], l_refs[gi][0], tm, r))
        outs.append(_token_major(o_scr.at[gi], o_refs[gi][0].astype(F32), tm, r))
    _, _, gate_a = _split_mod(mod_a_ref, b, d)
    shift, scale, _ = _split_mod(mod_m_ref, b, d)
    lane = lax.broadcasted_iota(jnp.int32, (1, LANES), 1)
    n_slab = d // LANES
    rc = tm // ROUTE_ROW_CHUNKS
    ri = lax.broadcasted_iota(jnp.int32, (rc, rc), 0)
    ci = lax.broadcasted_iota(jnp.int32, (rc, rc), 1)
    before = jnp.where(ci < ri, 1.0, 0.0).astype(BF16)
    count = base_scr[0:1, :]

    for c in range(ROUTE_ROW_CHUNKS):
        rows = slice(c * rc, (c + 1) * rc)
        ls = [l[rows] for l in lses]
        m = jnp.maximum(jnp.maximum(ls[0], ls[1]), ls[2])
        es = [jnp.exp(l - m) for l in ls]
        inv = 1.0 / (es[0] + es[1] + es[2])
        scaled = []
        for gi in range(len(ATTN_GROUPS)):
            alpha = es[gi] * inv
            hi = alpha.astype(BF16)
            lo = (alpha - hi.astype(F32)).astype(BF16)
            wide = jnp.dot(hi, expand_ref[...], preferred_element_type=F32) + jnp.dot(
                lo, expand_ref[...], preferred_element_type=F32)
            scaled.append((outs[gi][rows] * wide).astype(BF16))
        attn = jnp.dot(jnp.concatenate(scaled, axis=1), wo_ref[...], preferred_element_type=F32)
        x = x_ref[0, rows, :] + gate_a * attn
        x_out_ref[0, rows, :] = x

        h = _norm_mod(x, g_ref[...], scale, shift)
        for k in range(n_slab):
            h_out_ref[pl.ds(c * rc * n_slab + k, rc, stride=n_slab), :] = h[:, k * LANES:(k + 1) * LANES]

        h_hi = h.astype(BF16)
        h_lo = (h - h_hi.astype(F32)).astype(BF16)
        logits = (jnp.dot(h_hi, wr_ref[0], preferred_element_type=F32)
                  + jnp.dot(h_lo, wr_ref[0], preferred_element_type=F32)
                  + jnp.dot(h_hi, wr_ref[1], preferred_element_type=F32))
        logits = jnp.where(lane < n_exp, logits, NEG_INF)
        m1 = jnp.max(logits, axis=-1, keepdims=True)
        i1 = jnp.min(jnp.where(logits == m1, lane, LANES), axis=-1, keepdims=True)
        rest = jnp.where(lane == i1, NEG_INF, logits)
        m2 = jnp.max(rest, axis=-1, keepdims=True)
        i2 = jnp.min(jnp.where(rest == m2, lane, LANES), axis=-1, keepdims=True)
        e2 = jnp.exp(m2 - m1)
        w1 = 1.0 / (1.0 + e2)
        w2 = e2 * w1
        wts_ref[rows, :] = jnp.where(lane == 0, w1, jnp.where(lane == 1, w2, 0.0))

        oh1, oh2 = (lane == i1), (lane == i2)
        picks = jnp.where(oh1 | oh2, 1.0, 0.0).astype(BF16)
        rank = jnp.dot(before, picks, preferred_element_type=F32) + count
        r1 = jnp.sum(jnp.where(oh1, rank, 0.0), axis=-1, keepdims=True).astype(jnp.int32)
        r2 = jnp.sum(jnp.where(oh2, rank, 0.0), axis=-1, keepdims=True).astype(jnp.int32)
        route_ref[rows, :] = jnp.where(lane == 0, r1, jnp.where(lane == 1, r2, jnp.where(
            lane == 2, i1, jnp.where(lane == 3, i2, 0))))
        count = count + jnp.sum(picks.astype(F32), axis=0, keepdims=True)

    base_scr[...] = jnp.broadcast_to(count, base_scr.shape)
    cnt_ref[...] = jnp.broadcast_to(count, cnt_ref.shape).astype(jnp.int32)


def _attn_out_route(x, mods, set_a, set_m, g, outs, lses, expand, w_o, w_router, tm):
    bsz, s, d = x.shape
    t = bsz * s
    n_i = s // tm
    n_groups = len(ATTN_GROUPS)
    gl = GROUP_LANES
    tok = lambda b, i: (b, i, 0)
    flat = lambda b, i: (b * n_i + i, 0)
    const2 = lambda b, i: (0, 0)
    return pl.pallas_call(
        _attn_out_route_kernel,
        out_shape=[
            jax.ShapeDtypeStruct((bsz, s, d), F32),
            jax.ShapeDtypeStruct((t * ROW_SLAB, LANES), F32),
            jax.ShapeDtypeStruct((t, LANES), jnp.int32),
            jax.ShapeDtypeStruct((t, LANES), F32),
            jax.ShapeDtypeStruct((SUBLANES, LANES), jnp.int32),
        ],
        grid=(bsz, n_i),
        in_specs=[
            pl.BlockSpec((1, tm, d), tok),
            pl.BlockSpec((1, SUBLANES, 3 * d), lambda b, i: (set_a, 0, 0)),
            pl.BlockSpec((1, SUBLANES, 3 * d), lambda b, i: (set_m, 0, 0)),
            pl.BlockSpec((1, d), const2),
        ] + [pl.BlockSpec((1, tm, gl), tok)] * n_groups
          + [pl.BlockSpec((1, tm, LANES), tok)] * n_groups + [
            pl.BlockSpec((LANES, gl), const2),
            pl.BlockSpec((n_groups * gl, d), const2),
            pl.BlockSpec((2, d, LANES), lambda b, i: (0, 0, 0)),
        ],
        out_specs=[
            pl.BlockSpec((1, tm, d), tok),
            pl.BlockSpec((tm * ROW_SLAB, LANES), flat),
            pl.BlockSpec((tm, LANES), flat),
            pl.BlockSpec((tm, LANES), flat),
            pl.BlockSpec((SUBLANES, LANES), const2),
        ],
        scratch_shapes=[pltpu.VMEM((n_groups, gl // LANES, tm, LANES), F32),
                        pltpu.VMEM((n_groups, 1, tm, LANES), F32),
                        pltpu.VMEM((SUBLANES, LANES), F32)],
        compiler_params=_cparams(("arbitrary", "arbitrary")),
        name="attn_out_route",
    )(x, mods, mods, g, *outs, *lses, expand, w_o, w_router)


DMA_LOOP_UNROLL = 16


def _for_each(n, fn):
    def body(g, carry):
        for u in range(DMA_LOOP_UNROLL):
            fn(g * DMA_LOOP_UNROLL + u)
        return carry
    lax.fori_loop(0, n // DMA_LOOP_UNROLL, body, 0)


def _row_copy(src, src_row, dst, dst_row, sem):
    return pltpu.make_async_copy(src.at[pl.ds(pl.multiple_of(src_row * ROW_SLAB, ROW_SLAB), ROW_SLAB)],
                                 dst.at[pl.ds(pl.multiple_of(dst_row * ROW_SLAB, ROW_SLAB), ROW_SLAB)],
                                 sem)


def _dispatch_kernel(fill_ref, pos_ref, h_ref, xs_hbm, zeros_scr, sem, zsem, *, tmd, te, n_exp,
                     n_tiles):
    step = pl.program_id(0)

    def zero_fills():
        def zero_copy(first_row, n):
            start = pl.multiple_of(first_row * ROW_SLAB, ROW_SLAB)
            return pltpu.make_async_copy(zeros_scr.at[pl.ds(0, n * ROW_SLAB)],
                                         xs_hbm.at[pl.ds(start, n * ROW_SLAB)], zsem)
        fills = []
        for e in range(n_exp):
            row, length = fill_ref[e], fill_ref[n_exp + e]
            for bit in reversed(range(te.bit_length() - 1)):
                take = (length & (1 << bit)) != 0
                fills.append((take, zero_copy(row, 1 << bit)))
                row = row + jnp.where(take, 1 << bit, 0)
        n_used = fill_ref[2 * n_exp]
        for k in range(n_exp):
            fills.append((n_used + k < n_tiles, zero_copy((n_used + k) * te, te)))
        return fills

    @pl.when(step == 0)
    def _():
        zeros_scr[...] = jnp.zeros_like(zeros_scr)
        for take, cp in zero_fills():
            pl.when(take)(cp.start)
        for take, cp in zero_fills():
            pl.when(take)(cp.wait)

    def copies(tok):
        return [_row_copy(h_ref, tok, xs_hbm, pos_ref[0, 0, tok * TOP_K + k], sem)
                for k in range(TOP_K)]

    _for_each(tmd, lambda tok: [cp.start(priority=k) for k, cp in enumerate(copies(tok))])
    _for_each(tmd, lambda tok: [cp.wait() for cp in copies(tok)])


def _moe_dispatch(fill, pos_blocks, h_rows, n_tiles, n_exp, tmd, te):
    n_steps = pos_blocks.shape[0]
    assert te & (te - 1) == 0
    kernel = functools.partial(_dispatch_kernel, tmd=tmd, te=te, n_exp=n_exp, n_tiles=n_tiles)
    return pl.pallas_call(
        kernel,
        out_shape=jax.ShapeDtypeStruct((n_tiles * te * ROW_SLAB, LANES), F32),
        grid_spec=pltpu.PrefetchScalarGridSpec(
            num_scalar_prefetch=1,
            grid=(n_steps,),
            in_specs=[
                pl.BlockSpec((1, 1, tmd * TOP_K), lambda s, cnt: (s, 0, 0), memory_space=pltpu.SMEM),
                pl.BlockSpec((tmd * ROW_SLAB, LANES), lambda s, cnt: (s, 0)),
            ],
            out_specs=pl.BlockSpec(memory_space=pl.ANY),
            scratch_shapes=[pltpu.VMEM((te * ROW_SLAB, LANES), F32), pltpu.SemaphoreType.DMA,
                            pltpu.SemaphoreType.DMA],
        ),
        compiler_params=_cparams(("arbitrary",)),
        name="moe_dispatch",
    )(fill, pos_blocks, h_rows)


def _slab_rows_to_matrix(ref, n_rows, n_chunks):
    return jnp.concatenate([ref[pl.ds(c, n_rows, stride=n_chunks), :] for c in range(n_chunks)],
                           axis=1)


def _experts_kernel(tile_expert_ref, n_used_ref, x_ref, wg_ref, wu_ref, wd_ref, y_ref, *, te,
                    f_chunks):
    del tile_expert_ref
    used = pl.program_id(0) < n_used_ref[0]

    @pl.when(jnp.logical_not(used))
    def _():
        y_ref[...] = jnp.zeros_like(y_ref)

    @pl.when(used)
    def _():
        d = wg_ref.shape[1]
        n_chunks = d // LANES
        x = _slab_rows_to_matrix(x_ref, te, n_chunks).astype(BF16)
        y = jnp.zeros((te, d), F32)
        for lo, hi in f_chunks:
            gate = jnp.dot(x, wg_ref[0, :, lo:hi], preferred_element_type=F32)
            up = jnp.dot(x, wu_ref[0, :, lo:hi], preferred_element_type=F32)
            y = y + jnp.dot((_silu(gate) * up).astype(BF16), wd_ref[0, lo:hi, :],
                            preferred_element_type=F32)
        for c in range(n_chunks):
            y_ref[pl.ds(c, te, stride=n_chunks), :] = y[:, c * LANES:(c + 1) * LANES]


def _f_chunks(f):
    if f <= 1536:
        return ((0, f),)
    half = (f // 2 + 255) // 256 * 256
    return ((0, half), (half, f))


def _moe_experts(tile_expert, n_used, xs, w_gate, w_up, w_down, te):
    n_tiles = tile_expert.shape[0]
    _, d, f = w_gate.shape
    kernel = functools.partial(_experts_kernel, te=te, f_chunks=_f_chunks(f))
    weights = lambda i, te_, nu_: (te_[i], 0, 0)
    return pl.pallas_call(
        kernel,
        out_shape=jax.ShapeDtypeStruct(xs.shape, F32),
        grid_spec=pltpu.PrefetchScalarGridSpec(
            num_scalar_prefetch=2,
            grid=(n_tiles,),
            in_specs=[
                pl.BlockSpec((te * ROW_SLAB, LANES), lambda i, te_, nu_: (jnp.minimum(i, nu_[0] - 1), 0)),
                pl.BlockSpec((1, d, f), weights),
                pl.BlockSpec((1, d, f), weights),
                pl.BlockSpec((1, f, d), weights),
            ],
            out_specs=pl.BlockSpec((te * ROW_SLAB, LANES), lambda i, te_, nu_: (i, 0)),
        ),
        compiler_params=_cparams(("arbitrary",)),
        name="moe_experts",
    )(tile_expert, n_used, xs, w_gate, w_up, w_down)


def _combine_kernel(pos_ref, pos_next_ref, x_ref, mod_ref, g_ref, wts_ref, ys_hbm, o_ref, buf, sems,
                    *, tmc):
    b = pl.program_id(0)
    step = b * pl.num_programs(1) + pl.program_id(1)
    n_steps = pl.num_programs(0) * pl.num_programs(1)
    slot = step % 2
    d = x_ref.shape[2]
    n_chunks = d // LANES

    def copies(p_ref, slot_, tok):
        return [_row_copy(ys_hbm, p_ref[0, 0, tok * TOP_K + k], buf.at[slot_], k * tmc + tok,
                          sems.at[slot_]) for k in range(TOP_K)]

    def fetch(p_ref, slot_):
        _for_each(tmc, lambda tok: [cp.start(priority=k)
                                    for k, cp in enumerate(copies(p_ref, slot_, tok))])

    pl.when(step == 0)(lambda: fetch(pos_ref, slot))
    pl.when(step + 1 < n_steps)(lambda: fetch(pos_next_ref, 1 - slot))
    _for_each(tmc, lambda tok: [cp.wait() for cp in copies(pos_ref, slot, tok)])

    wts = wts_ref[...]
    y = jnp.zeros((tmc, d), F32)
    rows = buf.at[slot]
    for k in range(TOP_K):
        yk = _slab_rows_to_matrix(rows.at[pl.ds(k * tmc * ROW_SLAB, tmc * ROW_SLAB)], tmc, n_chunks)
        y = y + wts[:, k:k + 1] * yk
    _, _, gate = _split_mod(mod_ref, b, d)
    x = x_ref[0] + gate * y
    o_ref[0] = x * lax.rsqrt(jnp.mean(x * x, axis=-1, keepdims=True) + RMS_EPS) * g_ref[...]


def _moe_combine(pos_blocks, x, mods, set_idx, final_g, wts, ys, tmc):
    bsz, s, d = x.shape
    n_i = s // tmc
    last = bsz * n_i - 1
    kernel = functools.partial(_combine_kernel, tmc=tmc)
    return pl.pallas_call(
        kernel,
        out_shape=jax.ShapeDtypeStruct(x.shape, F32),
        grid=(bsz, n_i),
        in_specs=[
            pl.BlockSpec((1, 1, tmc * TOP_K), lambda b, i: (b * n_i + i, 0, 0),
                         memory_space=pltpu.SMEM),
            pl.BlockSpec((1, 1, tmc * TOP_K), lambda b, i: (jnp.minimum(b * n_i + i + 1, last), 0, 0),
                         memory_space=pltpu.SMEM),
            pl.BlockSpec((1, tmc, d), lambda b, i: (b, i, 0)),
            pl.BlockSpec((1, SUBLANES, 3 * d), lambda b, i: (set_idx, 0, 0)),
            pl.BlockSpec((1, d), lambda b, i: (0, 0)),
            pl.BlockSpec((tmc, LANES), lambda b, i: (b * n_i + i, 0)),
            pl.BlockSpec(memory_space=pl.ANY),
        ],
        out_specs=pl.BlockSpec((1, tmc, d), lambda b, i: (b, i, 0)),
        scratch_shapes=[pltpu.VMEM((2, TOP_K * tmc * ROW_SLAB, LANES), F32),
                        pltpu.SemaphoreType.DMA((2,))],
        compiler_params=_cparams(("arbitrary", "arbitrary")),
        name="moe_combine",
    )(pos_blocks, pos_blocks, x, mods, final_g, wts, ys)


def _pad_group_columns(w, which):
    width = HEADS_PER_GROUP * HEAD_DIM
    n_groups = len(ATTN_GROUPS)
    cols = w[:, which * n_groups * width:(which + 1) * n_groups * width]
    cols = cols.reshape(w.shape[0], n_groups, width)
    return jnp.pad(cols, ((0, 0), (0, 0), (0, GROUP_LANES - width)))


def kernel(x, c, positions, mod_w, mod_b, norm_g, conv_w_in, conv_w, conv_w_out, ffn_w_gate,
           ffn_w_up, ffn_w_down, attn_w_qkv, attn_w_o, router_w, moe_w_gate, moe_w_up, moe_w_down,
           final_g):
    bsz, s, d = x.shape
    t = bsz * s
    n_groups = len(ATTN_GROUPS)
    n_exp = router_w.shape[-1]
    assert all(w // r == ATTN_STEPS for w, r in ATTN_GROUPS)
    assert bsz <= SUBLANES and n_exp <= SUBLANES and d == ROW_SLAB * LANES
    tm = min(512, s)
    assert s % tm == 0 and all(s % (r * ATTN_STEPS) == 0 and tm % r == 0 for _, r in ATTN_GROUPS)

    c_pad = jnp.pad(c.astype(F32), ((0, SUBLANES - bsz), (0, 0)))
    mods = _ada_params(c_pad, mod_w.reshape(-1, d, 3 * d), mod_b.reshape(-1, 1, 3 * d))

    x = _conv_mixer(x, mods, 0, norm_g[0, 0][None], conv_w_in[0].astype(BF16), conv_w[0],
                    conv_w_out[0].astype(BF16), tm)
    x = _dense_ffn(x, mods, 1, norm_g[0, 1][None], ffn_w_gate[0].astype(BF16),
                   ffn_w_up[0].astype(BF16), ffn_w_down[0].astype(BF16), tm)

    w_qkv = attn_w_qkv[0]
    w_groups = jnp.concatenate([_pad_group_columns(w_qkv, which) for which in range(3)], axis=2)
    w_groups = w_groups.transpose(1, 0, 2).astype(BF16)
    inv_freq = ROPE_THETA ** (-jnp.arange(0, HEAD_DIM, 2, dtype=F32) / HEAD_DIM)
    freq = jnp.tile(inv_freq, LANES // (HEAD_DIM // 2))[None]
    qkvs = _qkv_rope(x, mods, 2, norm_g[1, 0][None], positions[..., None], freq, w_groups, tm)
    outs, lses = zip(*[_dilated_attn(qkv, r, tm) for qkv, (_, r) in zip(qkvs, ATTN_GROUPS)])

    width = HEADS_PER_GROUP * HEAD_DIM
    w_o = jnp.pad(attn_w_o[0].reshape(n_groups, width, d), ((0, 0), (0, GROUP_LANES - width), (0, 0)))
    w_o = w_o.reshape(n_groups * GROUP_LANES, d).astype(BF16)
    expand = (jnp.arange(GROUP_LANES)[None, :] // HEAD_DIM == jnp.arange(LANES)[:, None]).astype(BF16)
    w_router = jnp.pad(router_w[0], ((0, 0), (0, LANES - n_exp)))
    w_router_hi = w_router.astype(BF16)
    w_router = jnp.stack([w_router_hi, (w_router - w_router_hi.astype(F32)).astype(BF16)])

    x, h_rows, route, wts, counts = _attn_out_route(
        x, mods, 2, 3, norm_g[1, 1][None], outs, lses, expand, w_o, w_router, tm)

    te = min(512, t)
    n_tiles = (t * TOP_K) // te + n_exp
    counts = counts[0, :n_exp]
    tiles_per_expert = (counts + te - 1) // te
    ends = jnp.cumsum(tiles_per_expert)
    first_row = (ends - tiles_per_expert) * te
    n_used = ends[-1:]
    pos_flat = (first_row[route[:, TOP_K:2 * TOP_K]] + route[:, :TOP_K]).reshape(-1)
    fill = jnp.concatenate([first_row + counts, tiles_per_expert * te - counts, n_used]).astype(jnp.int32)
    idx = jnp.minimum(jnp.arange(n_tiles, dtype=jnp.int32), n_used - 1)
    tile_expert = jnp.sum(idx[:, None] >= ends[None, :], axis=1).astype(jnp.int32)

    pos_blocks = pos_flat.reshape(t // tm, 1, tm * TOP_K)
    xs = _moe_dispatch(fill, pos_blocks, h_rows, n_tiles, n_exp, tm, te)
    ys = _moe_experts(tile_expert, n_used.astype(jnp.int32), xs, moe_w_gate[0].astype(BF16),
                      moe_w_up[0].astype(BF16), moe_w_down[0].astype(BF16), te)
    return _moe_combine(pos_blocks, x, mods, 3, final_g[None], wts, ys, tm)
```

```python
import functools
import math

import jax
import jax.numpy as jnp
from jax import lax
from jax.experimental import pallas as pl
from jax.experimental.pallas import tpu as pltpu

CONV_WIDTH = 3
ATTN_GROUPS = ((128, 1), (512, 4), (2048, 16))
HEADS_PER_GROUP = 5
HEAD_DIM = 64
ROPE_THETA = 10000.0
TOP_K = 2
RMS_EPS = 1e-6

LANES = 128
SUBLANES = 8
VMEM_LIMIT_BYTES = 56 * 1024 * 1024

GROUP_LANES = 384
PAIRS_PER_GROUP = GROUP_LANES // LANES
ATTN_STEPS = 128
ROW_SLAB = 8
ROUTE_ROW_CHUNKS = 2

F32 = jnp.float32
BF16 = jnp.bfloat16
NEG_INF = float("-inf")


def _cparams(semantics):
    return pltpu.CompilerParams(dimension_semantics=semantics, vmem_limit_bytes=VMEM_LIMIT_BYTES)


def _norm_mod(x, g, scale, shift):
    y = x * lax.rsqrt(jnp.mean(x * x, axis=-1, keepdims=True) + RMS_EPS)
    return (y * g) * (1.0 + scale) + shift


def _split_mod(mod_ref, b, d):
    row = mod_ref[0, pl.ds(b, 1), :]
    return row[:, :d], row[:, d:2 * d], row[:, 2 * d:]


def _silu(x):
    return x * jax.nn.sigmoid(x)


def _ada_kernel(c_ref, w_ref, b_ref, o_ref):
    s = _silu(c_ref[...])
    o_ref[0] = jnp.dot(s, w_ref[0], precision=lax.Precision.HIGHEST,
                       preferred_element_type=F32) + b_ref[0]


def _ada_params(c_pad, mod_w, mod_b):
    n_sets, d, d3 = mod_w.shape
    tn = d
    return pl.pallas_call(
        _ada_kernel,
        out_shape=jax.ShapeDtypeStruct((n_sets, SUBLANES, d3), F32),
        grid=(n_sets, d3 // tn),
        in_specs=[
            pl.BlockSpec((SUBLANES, d), lambda s, j: (0, 0)),
            pl.BlockSpec((1, d, tn), lambda s, j: (s, 0, j)),
            pl.BlockSpec((1, 1, tn), lambda s, j: (s, 0, j)),
        ],
        out_specs=pl.BlockSpec((1, SUBLANES, tn), lambda s, j: (s, 0, j)),
        compiler_params=_cparams(("parallel", "parallel")),
        name="ada_params",
    )(c_pad, mod_w, mod_b)


def _conv_mixer_kernel(x_ref, mod_ref, g_ref, win_ref, cw_ref, wout_ref, o_ref, carry_ref):
    b, i = pl.program_id(0), pl.program_id(1)
    tm, d = x_ref.shape[1], x_ref.shape[2]

    @pl.when(i == 0)
    def _():
        carry_ref[...] = jnp.zeros_like(carry_ref)

    x = x_ref[0]
    shift, scale, gate = _split_mod(mod_ref, b, d)
    h = _norm_mod(x, g_ref[...], scale, shift).astype(BF16)
    bcu = jnp.dot(h, win_ref[...], preferred_element_type=F32)
    b_gate, v = bcu[:, :d], bcu[:, d:2 * d] * bcu[:, 2 * d:]
    row = lax.broadcasted_iota(jnp.int32, (tm, 1), 0)
    prev = carry_ref[...]
    v1 = jnp.where(row == 0, prev[7:8], pltpu.roll(v, 1, 0))
    v2 = jnp.where(row == 0, prev[6:7], jnp.where(row == 1, prev[7:8], pltpu.roll(v, 2, 0)))
    cw = cw_ref[...]
    conv = cw[0:1] * v2 + cw[1:2] * v1 + cw[2:3] * v
    carry_ref[...] = v[tm - SUBLANES:, :]
    y = jnp.dot((b_gate * conv).astype(BF16), wout_ref[...], preferred_element_type=F32)
    o_ref[0] = x + gate * y


def _conv_mixer(x, mods, set_idx, g, w_in, conv_w, w_out, tm):
    bsz, s, d = x.shape
    return pl.pallas_call(
        _conv_mixer_kernel,
        out_shape=jax.ShapeDtypeStruct(x.shape, F32),
        grid=(bsz, s // tm),
        in_specs=[
            pl.BlockSpec((1, tm, d), lambda b, i: (b, i, 0)),
            pl.BlockSpec((1, SUBLANES, 3 * d), lambda b, i: (set_idx, 0, 0)),
            pl.BlockSpec((1, d), lambda b, i: (0, 0)),
            pl.BlockSpec((d, 3 * d), lambda b, i: (0, 0)),
            pl.BlockSpec((CONV_WIDTH, d), lambda b, i: (0, 0)),
            pl.BlockSpec((d, d), lambda b, i: (0, 0)),
        ],
        out_specs=pl.BlockSpec((1, tm, d), lambda b, i: (b, i, 0)),
        scratch_shapes=[pltpu.VMEM((SUBLANES, d), F32)],
        compiler_params=_cparams(("arbitrary", "arbitrary")),
        name="conv_mixer",
    )(x, mods, g, w_in, conv_w, w_out)


def _ffn_kernel(x_ref, mod_ref, g_ref, wg_ref, wu_ref, wd_ref, o_ref):
    b = pl.program_id(0)
    d = x_ref.shape[2]
    x = x_ref[0]
    shift, scale, gate = _split_mod(mod_ref, b, d)
    h = _norm_mod(x, g_ref[...], scale, shift).astype(BF16)
    a = _silu(jnp.dot(h, wg_ref[...], preferred_element_type=F32)) * jnp.dot(
        h, wu_ref[...], preferred_element_type=F32)
    y = jnp.dot(a.astype(BF16), wd_ref[...], preferred_element_type=F32)
    o_ref[0] = x + gate * y


def _dense_ffn(x, mods, set_idx, g, w_gate, w_up, w_down, tm):
    bsz, s, d = x.shape
    f = w_gate.shape[1]
    resident = pl.Buffered(1)
    return pl.pallas_call(
        _ffn_kernel,
        out_shape=jax.ShapeDtypeStruct(x.shape, F32),
        grid=(bsz, s // tm),
        in_specs=[
            pl.BlockSpec((1, tm, d), lambda b, i: (b, i, 0)),
            pl.BlockSpec((1, SUBLANES, 3 * d), lambda b, i: (set_idx, 0, 0)),
            pl.BlockSpec((1, d), lambda b, i: (0, 0)),
            pl.BlockSpec((d, f), lambda b, i: (0, 0), pipeline_mode=resident),
            pl.BlockSpec((d, f), lambda b, i: (0, 0), pipeline_mode=resident),
            pl.BlockSpec((f, d), lambda b, i: (0, 0), pipeline_mode=resident),
        ],
        out_specs=pl.BlockSpec((1, tm, d), lambda b, i: (b, i, 0)),
        compiler_params=_cparams(("parallel", "parallel")),
        name="dense_ffn",
    )(x, mods, g, w_gate, w_up, w_down)


def _class_major(ref, tm, r_from, r_to):
    n_chunks = ref.shape[0]
    if r_to == r_from:
        return jnp.concatenate([ref[c] for c in range(n_chunks)], axis=1)

    def rows(res):
        start = (res % r_from) * (tm // r_from) + res // r_from
        return pl.ds(start, tm // r_to, stride=r_to // r_from)

    return jnp.concatenate(
        [jnp.concatenate([ref[c, rows(res), :] for c in range(n_chunks)], axis=1)
         for res in range(r_to)], axis=0)


def _split_matmul(a, b01):
    hi = a.astype(BF16)
    lo = (a - hi.astype(F32)).astype(BF16)
    return jnp.dot(hi, b01, preferred_element_type=F32) + jnp.dot(lo, b01, preferred_element_type=F32)


def _rotary_tables(pos, freq, tm):
    half = HEAD_DIM // 2
    per_row = LANES // half
    q4 = tm // per_row
    lane = lax.broadcasted_iota(jnp.int32, (1, LANES), 1)
    pos4 = pos[(per_row - 1) * q4:]
    for j in reversed(range(per_row - 1)):
        pos4 = jnp.where(lane // half == j, pos[j * q4:(j + 1) * q4], pos4)
    ang = pos4 * freq
    cos4, sin4 = jnp.cos(ang), jnp.sin(ang)
    li = lax.broadcasted_iota(jnp.int32, (LANES, LANES), 0)
    lj = lax.broadcasted_iota(jnp.int32, (LANES, LANES), 1)
    same_freq = li % half == lj % half
    sign = jnp.where(lj % HEAD_DIM < half, -1.0, 1.0)
    cos_rows, sin_rows = [], []
    for j in range(per_row):
        pick = same_freq & (li // half == j)
        cos_rows.append(_split_matmul(cos4, jnp.where(pick, 1.0, 0.0).astype(BF16)))
        sin_rows.append(_split_matmul(sin4, jnp.where(pick, sign, 0.0).astype(BF16)))
    return jnp.concatenate(cos_rows, axis=0), jnp.concatenate(sin_rows, axis=0)


def _store_lane_chunks(ref, val):
    for c in range(ref.shape[0]):
        ref[c] = val[:, c * LANES:(c + 1) * LANES]


def _rope(t, cos, sin_signed, first_half):
    out = []
    for c in range(t.shape[1] // LANES):
        tc = t[:, c * LANES:(c + 1) * LANES]
        rot = jnp.where(first_half, pltpu.roll(tc, LANES - HEAD_DIM // 2, 1),
                        pltpu.roll(tc, HEAD_DIM // 2, 1))
        out.append(tc * cos + rot * sin_signed)
    return jnp.concatenate(out, axis=1)


def _qkv_kernel(x_ref, mod_ref, g_ref, pos_ref, freq_ref, w_ref, o0_ref, o1_ref, o2_ref,
                scr_a, scr_b):
    b = pl.program_id(0)
    tm, d = x_ref.shape[1], x_ref.shape[2]
    shift, scale, _ = _split_mod(mod_ref, b, d)
    first_half = (lax.broadcasted_iota(jnp.int32, (1, LANES), 1) % HEAD_DIM) < HEAD_DIM // 2
    cos_t, sin_t = _rotary_tables(pos_ref[0].astype(F32), freq_ref[...], tm)
    _store_lane_chunks(scr_a, jnp.concatenate(
        [_norm_mod(x_ref[0], g_ref[...], scale, shift), cos_t, sin_t], axis=1))
    scrs, cur, cur_r = (scr_a, scr_b), 0, 1
    gl = GROUP_LANES
    for gi, ((_, r), o_ref) in enumerate(zip(ATTN_GROUPS, (o0_ref, o1_ref, o2_ref))):
        vals = _class_major(scrs[cur], tm, cur_r, r)
        if r != cur_r and gi + 1 < len(ATTN_GROUPS):
            cur, cur_r = 1 - cur, r
            _store_lane_chunks(scrs[cur], vals)
        hg, cos, sin_signed = vals[:, :d].astype(BF16), vals[:, d:d + LANES], vals[:, d + LANES:]
        qkv = jnp.dot(hg, w_ref[gi], preferred_element_type=F32)
        q = _rope(qkv[:, :gl], cos, sin_signed, first_half) * (HEAD_DIM ** -0.5)
        k = _rope(qkv[:, gl:2 * gl], cos, sin_signed, first_half)
        o_ref[0, :, :gl] = q.astype(BF16)
        o_ref[0, :, gl:2 * gl] = k.astype(BF16)
        o_ref[0, :, 2 * gl:] = qkv[:, 2 * gl:].astype(BF16)


def _qkv_rope(x, mods, set_idx, g, pos3, freq, w_groups, tm):
    bsz, s, d = x.shape
    n_groups = len(ATTN_GROUPS)
    out = jax.ShapeDtypeStruct((bsz, s, 3 * GROUP_LANES), BF16)
    return pl.pallas_call(
        _qkv_kernel,
        out_shape=[out] * n_groups,
        grid=(bsz, s // tm),
        in_specs=[
            pl.BlockSpec((1, tm, d), lambda b, i: (b, i, 0)),
            pl.BlockSpec((1, SUBLANES, 3 * d), lambda b, i: (set_idx, 0, 0)),
            pl.BlockSpec((1, d), lambda b, i: (0, 0)),
            pl.BlockSpec((1, tm, 1), lambda b, i: (b, i, 0)),
            pl.BlockSpec((1, LANES), lambda b, i: (0, 0)),
            pl.BlockSpec((n_groups, d, 3 * GROUP_LANES), lambda b, i: (0, 0, 0)),
        ],
        out_specs=[pl.BlockSpec((1, tm, 3 * GROUP_LANES), lambda b, i: (b, i, 0))] * n_groups,
        scratch_shapes=[pltpu.VMEM((d // LANES + 2, tm, LANES), F32)] * 2,
        compiler_params=_cparams(("parallel", "parallel")),
        name="qkv_rope",
    )(x, mods, g, pos3, freq, w_groups)


def _store_rows(ref, a, val):
    st, per_tile = ATTN_STEPS, ref.shape[1]
    if per_tile >= st:
        ref[(a * st) // per_tile, pl.ds((a * st) % per_tile, st), :] = val
    else:
        for j in range(st // per_tile):
            ref[a * (st // per_tile) + j] = val[j * per_tile:(j + 1) * per_tile]


def _attn_kernel(q_ref, kc_ref, vc_ref, kp_ref, vp_ref, o_ref, lse_ref):
    i = pl.program_id(2)
    st = ATTN_STEPS
    tq = q_ref.shape[0] * q_ref.shape[1]
    q = q_ref[...].reshape(tq, GROUP_LANES)
    k_rows = jnp.concatenate([kp_ref[...].reshape(st, GROUP_LANES),
                              kc_ref[...].reshape(tq, GROUP_LANES)], axis=0)
    v_rows = jnp.concatenate([vp_ref[...].reshape(st, GROUP_LANES),
                              vc_ref[...].reshape(tq, GROUP_LANES)], axis=0)
    lane = lax.broadcasted_iota(jnp.int32, (1, LANES), 1)
    qi = lax.broadcasted_iota(jnp.int32, (st, 2 * st), 0)
    kj = lax.broadcasted_iota(jnp.int32, (st, 2 * st), 1)
    band = (kj >= qi) & (kj <= qi + st)
    band_first = band & ((kj >= st) | (i > 0))
    nt_dims = (((1,), (1,)), ((), ()))
    zero = jnp.zeros((), BF16)

    k_heads, v_heads = [], []
    for pair in range(PAIRS_PER_GROUP):
        sl = slice(pair * LANES, (pair + 1) * LANES)
        kp_, vp_ = k_rows[:, sl], v_rows[:, sl]
        if HEADS_PER_GROUP - 2 * pair >= 2:
            halves = [lane < HEAD_DIM, lane >= HEAD_DIM]
            k_heads.append([jnp.where(m, kp_, zero) for m in halves])
            v_heads.append([jnp.where(m, vp_, zero) for m in halves])
        else:
            k_heads.append([kp_])
            v_heads.append([vp_])

    blocks = [(a, pair, hh) for a in range(tq // st) for pair in range(PAIRS_PER_GROUP)
              for hh in range(len(k_heads[pair]))]
    scores = []
    for a, pair, hh in blocks:
        qa = q[a * st:(a + 1) * st, pair * LANES:(pair + 1) * LANES]
        s = lax.dot_general(qa, k_heads[pair][hh][a * st:(a + 2) * st], nt_dims,
                            preferred_element_type=F32)
        scores.append(jnp.where(band if a else band_first, s, NEG_INF))
    s_all = jnp.concatenate(scores, axis=0)
    m_all = jnp.max(s_all, axis=-1, keepdims=True)
    p_all = jnp.exp(s_all - m_all)
    l_all = jnp.sum(p_all, axis=-1, keepdims=True)
    p_all = p_all.astype(BF16)
    inv_all = 1.0 / l_all
    lse_all = m_all + jnp.log(l_all)

    for a in range(tq // st):
        lse_tile = jnp.zeros((st, LANES), F32)
        o_tiles = []
        for pair in range(PAIRS_PER_GROUP):
            acc = jnp.zeros((st, LANES), F32)
            for hh in range(len(k_heads[pair])):
                rows = slice(blocks.index((a, pair, hh)) * st, (blocks.index((a, pair, hh)) + 1) * st)
                pv = jnp.dot(p_all[rows], v_heads[pair][hh][a * st:(a + 2) * st],
                             preferred_element_type=F32)
                acc = acc + pv * inv_all[rows]
                lse_tile = jnp.where(lane == 2 * pair + hh, lse_all[rows], lse_tile)
            o_tiles.append(acc.astype(BF16))
        _store_rows(o_ref, a, jnp.concatenate(o_tiles, axis=1))
        _store_rows(lse_ref, a, lse_tile)


def _dilated_attn(qkv, r, tmp):
    bsz, s, width = qkv.shape
    gl, st = GROUP_LANES, ATTN_STEPS
    cls_rows = tmp // r
    tq = min(512, s // r)
    nt = tq // cls_rows
    n_tiles = s // tmp
    n_q = (s // r) // tq
    sub = tq // st
    main = qkv.reshape(bsz, n_tiles, r, cls_rows, width)

    def main_spec(col):
        return pl.BlockSpec((None, nt, None, cls_rows, gl), lambda b, res, i: (b, i, res, 0, col))

    if cls_rows >= st:
        per_tile = cls_rows // st
        prev_arr = qkv.reshape(bsz, n_tiles, r, per_tile, st, width)

        def prev_spec(col):
            def index(b, res, i):
                c = jnp.maximum(i * sub - 1, 0)
                return (b, c // per_tile, res, c % per_tile, 0, col)
            return pl.BlockSpec((None, None, None, None, st, gl), index)
    else:
        ntp = st // cls_rows
        prev_arr = qkv.reshape(bsz, n_tiles // ntp, ntp, r, cls_rows, width)

        def prev_spec(col):
            return pl.BlockSpec((None, None, ntp, None, cls_rows, gl),
                                lambda b, res, i: (b, jnp.maximum(i * sub - 1, 0), 0, res, 0, col))

    o5 = jax.ShapeDtypeStruct((bsz, n_tiles, r, cls_rows, gl), BF16)
    l5 = jax.ShapeDtypeStruct((bsz, n_tiles, r, cls_rows, LANES), F32)

    o, lse = pl.pallas_call(
        _attn_kernel,
        out_shape=[o5, l5],
        grid=(bsz, r, n_q),
        in_specs=[main_spec(0), main_spec(1), main_spec(2), prev_spec(1), prev_spec(2)],
        out_specs=[
            pl.BlockSpec((None, nt, None, cls_rows, gl), lambda b, res, i: (b, i, res, 0, 0)),
            pl.BlockSpec((None, nt, None, cls_rows, LANES), lambda b, res, i: (b, i, res, 0, 0)),
        ],
        compiler_params=_cparams(("parallel", "parallel", "parallel")),
        name=f"dilated_attn_r{r}",
    )(main, main, main, prev_arr, prev_arr)
    return o.reshape(bsz, s, gl), lse.reshape(bsz, s, LANES)


def _token_major(scr, val, tm, r):
    if r == 1:
        return val
    n = tm // r
    for c in range(scr.shape[0]):
        for res in range(r):
            scr[c, pl.ds(res, n, stride=r), :] = val[res * n:(res + 1) * n, c * LANES:(c + 1) * LANES]
    return jnp.concatenate([scr[c] for c in range(scr.shape[0])], axis=1)


def _attn_out_route_kernel(x_ref, mod_a_ref, mod_m_ref, g_ref, o0_ref, o1_ref, o2_ref,
                           l0_ref, l1_ref, l2_ref, expand_ref, wo_ref, wr_ref,
                           x_out_ref, h_out_ref, route_ref, wts_ref, cnt_ref,
                           o_scr, l_scr, base_scr):
    b, i = pl.program_id(0), pl.program_id(1)
    tm, d = x_ref.shape[1], x_ref.shape[2]
    n_exp = cnt_ref.shape[0]

    @pl.when((b == 0) & (i == 0))
    def _():
        base_scr[...] = jnp.zeros_like(base_scr)

    o_refs, l_refs = (o0_ref, o1_ref, o2_ref), (l0_ref, l1_ref, l2_ref)
    outs, lses = [], []
    for gi, (_, r) in enumerate(ATTN_GROUPS):
        lses.append(_token_major(l_scr.at[gi], l_refs[gi][0], tm, r))
        outs.append(_token_major(o_scr.at[gi], o_refs[gi][0].astype(F32), tm, r))
    _, _, gate_a = _split_mod(mod_a_ref, b, d)
    shift, scale, _ = _split_mod(mod_m_ref, b, d)
    lane = lax.broadcasted_iota(jnp.int32, (1, LANES), 1)
    n_slab = d // LANES
    rc = tm // ROUTE_ROW_CHUNKS
    ri = lax.broadcasted_iota(jnp.int32, (rc, rc), 0)
    ci = lax.broadcasted_iota(jnp.int32, (rc, rc), 1)
    before = jnp.where(ci < ri, 1.0, 0.0).astype(BF16)
    count = base_scr[0:1, :]

    for c in range(ROUTE_ROW_CHUNKS):
        rows = slice(c * rc, (c + 1) * rc)
        ls = [l[rows] for l in lses]
        m = jnp.maximum(jnp.maximum(ls[0], ls[1]), ls[2])
        es = [jnp.exp(l - m) for l in ls]
        inv = 1.0 / (es[0] + es[1] + es[2])
        scaled = []
        for gi in range(len(ATTN_GROUPS)):
            alpha = es[gi] * inv
            hi = alpha.astype(BF16)
            lo = (alpha - hi.astype(F32)).astype(BF16)
            wide = jnp.dot(hi, expand_ref[...], preferred_element_type=F32) + jnp.dot(
                lo, expand_ref[...], preferred_element_type=F32)
            scaled.append((outs[gi][rows] * wide).astype(BF16))
        attn = jnp.dot(jnp.concatenate(scaled, axis=1), wo_ref[...], preferred_element_type=F32)
        x = x_ref[0, rows, :] + gate_a * attn
        x_out_ref[0, rows, :] = x

        h = _norm_mod(x, g_ref[...], scale, shift)
        for k in range(n_slab):
            h_out_ref[pl.ds(c * rc * n_slab + k, rc, stride=n_slab), :] = h[:, k * LANES:(k + 1) * LANES]

        h_hi = h.astype(BF16)
        h_lo = (h - h_hi.astype(F32)).astype(BF16)
        logits = (jnp.dot(h_hi, wr_ref[0], preferred_element_type=F32)
                  + jnp.dot(h_lo, wr_ref[0], preferred_element_type=F32)
                  + jnp.dot(h_hi, wr_ref[1], preferred_element_type=F32))
        logits = jnp.where(lane < n_exp, logits, NEG_INF)
        m1 = jnp.max(logits, axis=-1, keepdims=True)
        i1 = jnp.min(jnp.where(logits == m1, lane, LANES), axis=-1, keepdims=True)
        rest = jnp.where(lane == i1, NEG_INF, logits)
        m2 = jnp.max(rest, axis=-1, keepdims=True)
        i2 = jnp.min(jnp.where(rest == m2, lane, LANES), axis=-1, keepdims=True)
        e2 = jnp.exp(m2 - m1)
        w1 = 1.0 / (1.0 + e2)
        w2 = e2 * w1
        wts_ref[rows, :] = jnp.where(lane == 0, w1, jnp.where(lane == 1, w2, 0.0))

        oh1, oh2 = (lane == i1), (lane == i2)
        picks = jnp.where(oh1 | oh2, 1.0, 0.0).astype(BF16)
        rank = jnp.dot(before, picks, preferred_element_type=F32) + count
        r1 = jnp.sum(jnp.where(oh1, rank, 0.0), axis=-1, keepdims=True).astype(jnp.int32)
        r2 = jnp.sum(jnp.where(oh2, rank, 0.0), axis=-1, keepdims=True).astype(jnp.int32)
        route = jnp.where(lane == 0, r1, jnp.where(lane == 1, r2, jnp.where(
            lane == 2, i1, jnp.where(lane == 3, i2, 0))))
        route_ref[:, rows] = jnp.transpose(route)[:SUBLANES, :]
        count = count + jnp.sum(picks.astype(F32), axis=0, keepdims=True)

    base_scr[...] = jnp.broadcast_to(count, base_scr.shape)
    cnt_ref[...] = jnp.broadcast_to(count, cnt_ref.shape).astype(jnp.int32)


def _attn_out_route(x, mods, set_a, set_m, g, outs, lses, expand, w_o, w_router, tm):
    bsz, s, d = x.shape
    t = bsz * s
    n_i = s // tm
    n_groups = len(ATTN_GROUPS)
    gl = GROUP_LANES
    tok = lambda b, i: (b, i, 0)
    flat = lambda b, i: (b * n_i + i, 0)
    const2 = lambda b, i: (0, 0)
    return pl.pallas_call(
        _attn_out_route_kernel,
        out_shape=[
            jax.ShapeDtypeStruct((bsz, s, d), F32),
            jax.ShapeDtypeStruct((t * ROW_SLAB, LANES), F32),
            jax.ShapeDtypeStruct((SUBLANES, t), jnp.int32),
            jax.ShapeDtypeStruct((t, LANES), F32),
            jax.ShapeDtypeStruct((SUBLANES, LANES), jnp.int32),
        ],
        grid=(bsz, n_i),
        in_specs=[
            pl.BlockSpec((1, tm, d), tok),
            pl.BlockSpec((1, SUBLANES, 3 * d), lambda b, i: (set_a, 0, 0)),
            pl.BlockSpec((1, SUBLANES, 3 * d), lambda b, i: (set_m, 0, 0)),
            pl.BlockSpec((1, d), const2),
        ] + [pl.BlockSpec((1, tm, gl), tok)] * n_groups
          + [pl.BlockSpec((1, tm, LANES), tok)] * n_groups + [
            pl.BlockSpec((LANES, gl), const2),
            pl.BlockSpec((n_groups * gl, d), const2),
            pl.BlockSpec((2, d, LANES), lambda b, i: (0, 0, 0)),
        ],
        out_specs=[
            pl.BlockSpec((1, tm, d), tok),
            pl.BlockSpec((tm * ROW_SLAB, LANES), flat),
            pl.BlockSpec((SUBLANES, tm), lambda b, i: (0, b * n_i + i)),
            pl.BlockSpec((tm, LANES), flat),
            pl.BlockSpec((SUBLANES, LANES), const2),
        ],
        scratch_shapes=[pltpu.VMEM((n_groups, gl // LANES, tm, LANES), F32),
                        pltpu.VMEM((n_groups, 1, tm, LANES), F32),
                        pltpu.VMEM((SUBLANES, LANES), F32)],
        compiler_params=_cparams(("arbitrary", "arbitrary")),
        name="attn_out_route",
    )(x, mods, mods, g, *outs, *lses, expand, w_o, w_router)


DMA_LOOP_UNROLL = 16


def _for_each(n, fn):
    def body(g, carry):
        for u in range(DMA_LOOP_UNROLL):
            fn(g * DMA_LOOP_UNROLL + u)
        return carry
    lax.fori_loop(0, n // DMA_LOOP_UNROLL, body, 0)


def _row_copy(src, src_row, dst, dst_row, sem):
    return pltpu.make_async_copy(src.at[pl.ds(pl.multiple_of(src_row * ROW_SLAB, ROW_SLAB), ROW_SLAB)],
                                 dst.at[pl.ds(pl.multiple_of(dst_row * ROW_SLAB, ROW_SLAB), ROW_SLAB)],
                                 sem)


def _dispatch_kernel(fill_ref, pos_ref, h_ref, wg_ref, wu_ref, wd_ref, xs_hbm, wg_out, wu_out,
                     wd_out, zeros_scr, sem, zsem, *, tmd, te, n_exp, n_tiles):
    step = pl.program_id(0)

    def zero_fills():
        def zero_copy(first_row, n):
            start = pl.multiple_of(first_row * ROW_SLAB, ROW_SLAB)
            return pltpu.make_async_copy(zeros_scr.at[pl.ds(0, n * ROW_SLAB)],
                                         xs_hbm.at[pl.ds(start, n * ROW_SLAB)], zsem)
        fills = []
        for e in range(n_exp):
            row, length = fill_ref[e], fill_ref[n_exp + e]
            for bit in reversed(range(te.bit_length() - 1)):
                take = (length & (1 << bit)) != 0
                fills.append((take, zero_copy(row, 1 << bit)))
                row = row + jnp.where(take, 1 << bit, 0)
        n_used = fill_ref[2 * n_exp]
        for k in range(n_exp):
            fills.append((n_used + k < n_tiles, zero_copy((n_used + k) * te, te)))
        return fills

    @pl.when(step == 0)
    def _():
        zeros_scr[...] = jnp.zeros_like(zeros_scr)
        for take, cp in zero_fills():
            pl.when(take)(cp.start)
        for take, cp in zero_fills():
            pl.when(take)(cp.wait)

    def copies(tok):
        return [_row_copy(h_ref, tok, xs_hbm, pos_ref[0, 0, k * tmd + tok], sem)
                for k in range(TOP_K)]

    _for_each(tmd, lambda tok: [cp.start(priority=k) for k, cp in enumerate(copies(tok))])
    for src, dst in ((wg_ref, wg_out), (wu_ref, wu_out), (wd_ref, wd_out)):
        dst[...] = src[...].astype(BF16)
    _for_each(tmd, lambda tok: [cp.wait() for cp in copies(tok)])


def _moe_dispatch(fill, pos_blocks, h_rows, w_gate, w_up, w_down, n_tiles, tmd, te):
    n_steps = pos_blocks.shape[0]
    n_exp, d, f = w_gate.shape
    per_expert = n_steps // n_exp
    assert te & (te - 1) == 0 and n_steps == per_expert * n_exp
    sublane_pack = 2 * SUBLANES
    assert d % (per_expert * sublane_pack) == 0 and f % (per_expert * sublane_pack) == 0
    kernel = functools.partial(_dispatch_kernel, tmd=tmd, te=te, n_exp=n_exp, n_tiles=n_tiles)
    w_slice = lambda s, cnt: (s // per_expert, s % per_expert, 0)
    up_spec = pl.BlockSpec((1, d // per_expert, f), w_slice)
    down_spec = pl.BlockSpec((1, f // per_expert, d), w_slice)
    return pl.pallas_call(
        kernel,
        out_shape=[jax.ShapeDtypeStruct((n_tiles * te * ROW_SLAB, LANES), F32),
                   jax.ShapeDtypeStruct(w_gate.shape, BF16), jax.ShapeDtypeStruct(w_up.shape, BF16),
                   jax.ShapeDtypeStruct(w_down.shape, BF16)],
        grid_spec=pltpu.PrefetchScalarGridSpec(
            num_scalar_prefetch=1,
            grid=(n_steps,),
            in_specs=[
                pl.BlockSpec((1, 1, tmd * TOP_K), lambda s, cnt: (s, 0, 0), memory_space=pltpu.SMEM),
                pl.BlockSpec((tmd * ROW_SLAB, LANES), lambda s, cnt: (s, 0)),
                up_spec, up_spec, down_spec,
            ],
            out_specs=[pl.BlockSpec(memory_space=pl.ANY), up_spec, up_spec, down_spec],
            scratch_shapes=[pltpu.VMEM((te * ROW_SLAB, LANES), F32), pltpu.SemaphoreType.DMA,
                            pltpu.SemaphoreType.DMA],
        ),
        compiler_params=_cparams(("arbitrary",)),
        name="moe_dispatch",
    )(fill, pos_blocks, h_rows, w_gate, w_up, w_down)


def _slab_rows_to_matrix(ref, n_rows, n_chunks):
    return jnp.concatenate([ref[pl.ds(c, n_rows, stride=n_chunks), :] for c in range(n_chunks)],
                           axis=1)


def _experts_kernel(tile_expert_ref, n_used_ref, x_ref, wg_ref, wu_ref, wd_ref, y_ref, *, te,
                    f_chunks):
    del tile_expert_ref
    used = pl.program_id(0) < n_used_ref[0]

    @pl.when(jnp.logical_not(used))
    def _():
        y_ref[...] = jnp.zeros_like(y_ref)

    @pl.when(used)
    def _():
        d = wg_ref.shape[1]
        n_chunks = d // LANES
        x = _slab_rows_to_matrix(x_ref, te, n_chunks).astype(BF16)
        y = jnp.zeros((te, d), F32)
        for lo, hi in f_chunks:
            gate = jnp.dot(x, wg_ref[0, :, lo:hi], preferred_element_type=F32)
            up = jnp.dot(x, wu_ref[0, :, lo:hi], preferred_element_type=F32)
            y = y + jnp.dot((_silu(gate) * up).astype(BF16), wd_ref[0, lo:hi, :],
                            preferred_element_type=F32)
        for c in range(n_chunks):
            y_ref[pl.ds(c, te, stride=n_chunks), :] = y[:, c * LANES:(c + 1) * LANES]


def _f_chunks(f):
    if f <= 1536:
        return ((0, f),)
    half = (f // 2 + 255) // 256 * 256
    return ((0, half), (half, f))


def _moe_experts(tile_expert, n_used, xs, w_gate, w_up, w_down, te):
    n_tiles = tile_expert.shape[0]
    _, d, f = w_gate.shape
    kernel = functools.partial(_experts_kernel, te=te, f_chunks=_f_chunks(f))
    weights = lambda i, te_, nu_: (te_[i], 0, 0)
    return pl.pallas_call(
        kernel,
        out_shape=jax.ShapeDtypeStruct(xs.shape, F32),
        grid_spec=pltpu.PrefetchScalarGridSpec(
            num_scalar_prefetch=2,
            grid=(n_tiles,),
            in_specs=[
                pl.BlockSpec((te * ROW_SLAB, LANES), lambda i, te_, nu_: (jnp.minimum(i, nu_[0] - 1), 0)),
                pl.BlockSpec((1, d, f), weights),
                pl.BlockSpec((1, d, f), weights),
                pl.BlockSpec((1, f, d), weights),
            ],
            out_specs=pl.BlockSpec((te * ROW_SLAB, LANES), lambda i, te_, nu_: (i, 0)),
        ),
        compiler_params=_cparams(("arbitrary",)),
        name="moe_experts",
    )(tile_expert, n_used, xs, w_gate, w_up, w_down)


def _combine_kernel(pos_ref, pos_next_ref, x_ref, mod_ref, g_ref, wts_ref, ys_hbm, o_ref, buf, sems,
                    *, tmc):
    b = pl.program_id(0)
    step = b * pl.num_programs(1) + pl.program_id(1)
    n_steps = pl.num_programs(0) * pl.num_programs(1)
    slot = step % 2
    d = x_ref.shape[2]
    n_chunks = d // LANES

    def copies(p_ref, slot_, tok):
        return [_row_copy(ys_hbm, p_ref[0, 0, k * tmc + tok], buf.at[slot_], k * tmc + tok,
                          sems.at[slot_]) for k in range(TOP_K)]

    def fetch(p_ref, slot_):
        _for_each(tmc, lambda tok: [cp.start(priority=k)
                                    for k, cp in enumerate(copies(p_ref, slot_, tok))])

    pl.when(step == 0)(lambda: fetch(pos_ref, slot))
    pl.when(step + 1 < n_steps)(lambda: fetch(pos_next_ref, 1 - slot))
    _for_each(tmc, lambda tok: [cp.wait() for cp in copies(pos_ref, slot, tok)])

    wts = wts_ref[...]
    y = jnp.zeros((tmc, d), F32)
    rows = buf.at[slot]
    for k in range(TOP_K):
        yk = _slab_rows_to_matrix(rows.at[pl.ds(k * tmc * ROW_SLAB, tmc * ROW_SLAB)], tmc, n_chunks)
        y = y + wts[:, k:k + 1] * yk
    _, _, gate = _split_mod(mod_ref, b, d)
    x = x_ref[0] + gate * y
    o_ref[0] = x * lax.rsqrt(jnp.mean(x * x, axis=-1, keepdims=True) + RMS_EPS) * g_ref[...]


def _moe_combine(pos_blocks, x, mods, set_idx, final_g, wts, ys, tmc):
    bsz, s, d = x.shape
    n_i = s // tmc
    last = bsz * n_i - 1
    kernel = functools.partial(_combine_kernel, tmc=tmc)
    return pl.pallas_call(
        kernel,
        out_shape=jax.ShapeDtypeStruct(x.shape, F32),
        grid=(bsz, n_i),
        in_specs=[
            pl.BlockSpec((1, 1, tmc * TOP_K), lambda b, i: (b * n_i + i, 0, 0),
                         memory_space=pltpu.SMEM),
            pl.BlockSpec((1, 1, tmc * TOP_K), lambda b, i: (jnp.minimum(b * n_i + i + 1, last), 0, 0),
                         memory_space=pltpu.SMEM),
            pl.BlockSpec((1, tmc, d), lambda b, i: (b, i, 0)),
            pl.BlockSpec((1, SUBLANES, 3 * d), lambda b, i: (set_idx, 0, 0)),
            pl.BlockSpec((1, d), lambda b, i: (0, 0)),
            pl.BlockSpec((tmc, LANES), lambda b, i: (b * n_i + i, 0)),
            pl.BlockSpec(memory_space=pl.ANY),
        ],
        out_specs=pl.BlockSpec((1, tmc, d), lambda b, i: (b, i, 0)),
        scratch_shapes=[pltpu.VMEM((2, TOP_K * tmc * ROW_SLAB, LANES), F32),
                        pltpu.SemaphoreType.DMA((2,))],
        compiler_params=_cparams(("arbitrary", "arbitrary")),
        name="moe_combine",
    )(pos_blocks, pos_blocks, x, mods, final_g, wts, ys)


def _pad_group_columns(w, which):
    width = HEADS_PER_GROUP * HEAD_DIM
    n_groups = len(ATTN_GROUPS)
    cols = w[:, which * n_groups * width:(which + 1) * n_groups * width]
    cols = cols.reshape(w.shape[0], n_groups, width)
    return jnp.pad(cols, ((0, 0), (0, 0), (0, GROUP_LANES - width)))


def kernel(x, c, positions, mod_w, mod_b, norm_g, conv_w_in, conv_w, conv_w_out, ffn_w_gate,
           ffn_w_up, ffn_w_down, attn_w_qkv, attn_w_o, router_w, moe_w_gate, moe_w_up, moe_w_down,
           final_g):
    bsz, s, d = x.shape
    t = bsz * s
    n_groups = len(ATTN_GROUPS)
    n_exp = router_w.shape[-1]
    assert all(w // r == ATTN_STEPS for w, r in ATTN_GROUPS)
    assert bsz <= SUBLANES and n_exp <= SUBLANES and d == ROW_SLAB * LANES
    tm = min(512, s)
    assert s % tm == 0 and all(s % (r * ATTN_STEPS) == 0 and tm % r == 0 for _, r in ATTN_GROUPS)

    c_pad = jnp.pad(c.astype(F32), ((0, SUBLANES - bsz), (0, 0)))
    mods = _ada_params(c_pad, mod_w.reshape(-1, d, 3 * d), mod_b.reshape(-1, 1, 3 * d))

    x = _conv_mixer(x, mods, 0, norm_g[0, 0][None], conv_w_in[0].astype(BF16), conv_w[0],
                    conv_w_out[0].astype(BF16), tm)
    x = _dense_ffn(x, mods, 1, norm_g[0, 1][None], ffn_w_gate[0].astype(BF16),
                   ffn_w_up[0].astype(BF16), ffn_w_down[0].astype(BF16), tm)

    w_qkv = attn_w_qkv[0]
    w_groups = jnp.concatenate([_pad_group_columns(w_qkv, which) for which in range(3)], axis=2)
    w_groups = w_groups.transpose(1, 0, 2).astype(BF16)
    inv_freq = ROPE_THETA ** (-jnp.arange(0, HEAD_DIM, 2, dtype=F32) / HEAD_DIM)
    freq = jnp.tile(inv_freq, LANES // (HEAD_DIM // 2))[None]
    qkvs = _qkv_rope(x, mods, 2, norm_g[1, 0][None], positions[..., None], freq, w_groups, tm)
    outs, lses = zip(*[_dilated_attn(qkv, r, tm) for qkv, (_, r) in zip(qkvs, ATTN_GROUPS)])

    width = HEADS_PER_GROUP * HEAD_DIM
    w_o = jnp.pad(attn_w_o[0].reshape(n_groups, width, d), ((0, 0), (0, GROUP_LANES - width), (0, 0)))
    w_o = w_o.reshape(n_groups * GROUP_LANES, d).astype(BF16)
    expand = (jnp.arange(GROUP_LANES)[None, :] // HEAD_DIM == jnp.arange(LANES)[:, None]).astype(BF16)
    w_router = jnp.pad(router_w[0], ((0, 0), (0, LANES - n_exp)))
    w_router_hi = w_router.astype(BF16)
    w_router = jnp.stack([w_router_hi, (w_router - w_router_hi.astype(F32)).astype(BF16)])

    x, h_rows, route, wts, counts = _attn_out_route(
        x, mods, 2, 3, norm_g[1, 1][None], outs, lses, expand, w_o, w_router, tm)

    te = min(512, t)
    n_tiles = (t * TOP_K) // te + n_exp
    counts = counts[0, :n_exp]
    tiles_per_expert = (counts + te - 1) // te
    ends = jnp.cumsum(tiles_per_expert)
    first_row = (ends - tiles_per_expert) * te
    n_used = ends[-1:]
    pos = first_row[route[TOP_K:2 * TOP_K]] + route[:TOP_K]
    fill = jnp.concatenate([first_row + counts, tiles_per_expert * te - counts, n_used]).astype(jnp.int32)
    idx = jnp.minimum(jnp.arange(n_tiles, dtype=jnp.int32), n_used - 1)
    tile_expert = jnp.sum(idx[:, None] >= ends[None, :], axis=1).astype(jnp.int32)

    pos_blocks = pos.reshape(TOP_K, t // tm, tm).transpose(1, 0, 2).reshape(t // tm, 1, TOP_K * tm)
    xs, w_gate, w_up, w_down = _moe_dispatch(fill, pos_blocks, h_rows, moe_w_gate[0], moe_w_up[0],
                                             moe_w_down[0], n_tiles, tm, te)
    ys = _moe_experts(tile_expert, n_used.astype(jnp.int32), xs, w_gate, w_up, w_down, te)
    return _moe_combine(pos_blocks, x, mods, 3, final_g[None], wts, ys, tm)
```

```python
import functools
import math

import jax
import jax.numpy as jnp
from jax import lax
from jax.experimental import pallas as pl
from jax.experimental.pallas import tpu as pltpu

CONV_WIDTH = 3
ATTN_GROUPS = ((128, 1), (512, 4), (2048, 16))
HEADS_PER_GROUP = 5
HEAD_DIM = 64
ROPE_THETA = 10000.0
TOP_K = 2
RMS_EPS = 1e-6

LANES = 128
SUBLANES = 8
VMEM_LIMIT_BYTES = 56 * 1024 * 1024

GROUP_LANES = 384
PAIRS_PER_GROUP = GROUP_LANES // LANES
ATTN_STEPS = 128
ROW_SLAB = 8
ROUTE_ROW_CHUNKS = 2

F32 = jnp.float32
BF16 = jnp.bfloat16
NEG_INF = float("-inf")


def _cparams(semantics):
    return pltpu.CompilerParams(dimension_semantics=semantics, vmem_limit_bytes=VMEM_LIMIT_BYTES)


def _norm_mod(x, g, scale, shift):
    y = x * lax.rsqrt(jnp.mean(x * x, axis=-1, keepdims=True) + RMS_EPS)
    return (y * g) * (1.0 + scale) + shift


def _split_mod(mod_ref, b, d):
    row = mod_ref[0, pl.ds(b, 1), :]
    return row[:, :d], row[:, d:2 * d], row[:, 2 * d:]


def _silu(x):
    return x * jax.nn.sigmoid(x)


def _ada_kernel(c_ref, w_ref, b_ref, o_ref):
    s = _silu(c_ref[...])
    o_ref[0] = jnp.dot(s, w_ref[0], precision=lax.Precision.HIGHEST,
                       preferred_element_type=F32) + b_ref[0]


def _ada_params(c_pad, mod_w, mod_b):
    n_sets, d, d3 = mod_w.shape
    tn = d
    return pl.pallas_call(
        _ada_kernel,
        out_shape=jax.ShapeDtypeStruct((n_sets, SUBLANES, d3), F32),
        grid=(n_sets, d3 // tn),
        in_specs=[
            pl.BlockSpec((SUBLANES, d), lambda s, j: (0, 0)),
            pl.BlockSpec((1, d, tn), lambda s, j: (s, 0, j)),
            pl.BlockSpec((1, 1, tn), lambda s, j: (s, 0, j)),
        ],
        out_specs=pl.BlockSpec((1, SUBLANES, tn), lambda s, j: (s, 0, j)),
        compiler_params=_cparams(("parallel", "parallel")),
        name="ada_params",
    )(c_pad, mod_w, mod_b)


def _conv_mixer_kernel(x_ref, mod_ref, g_ref, win_ref, cw_ref, wout_ref, eg_ref, eu_ref, ed_ref,
                       o_ref, eg_out, eu_out, ed_out, carry_ref):
    b, i = pl.program_id(0), pl.program_id(1)
    tm, d = x_ref.shape[1], x_ref.shape[2]
    for src, dst in ((eg_ref, eg_out), (eu_ref, eu_out), (ed_ref, ed_out)):
        dst[...] = src[...].astype(BF16)

    @pl.when(i == 0)
    def _():
        carry_ref[...] = jnp.zeros_like(carry_ref)

    x = x_ref[0]
    shift, scale, gate = _split_mod(mod_ref, b, d)
    h = _norm_mod(x, g_ref[...], scale, shift).astype(BF16)
    bcu = jnp.dot(h, win_ref[...], preferred_element_type=F32)
    b_gate, v = bcu[:, :d], bcu[:, d:2 * d] * bcu[:, 2 * d:]
    row = lax.broadcasted_iota(jnp.int32, (tm, 1), 0)
    prev = carry_ref[...]
    v1 = jnp.where(row == 0, prev[7:8], pltpu.roll(v, 1, 0))
    v2 = jnp.where(row == 0, prev[6:7], jnp.where(row == 1, prev[7:8], pltpu.roll(v, 2, 0)))
    cw = cw_ref[...]
    conv = cw[0:1] * v2 + cw[1:2] * v1 + cw[2:3] * v
    carry_ref[...] = v[tm - SUBLANES:, :]
    y = jnp.dot((b_gate * conv).astype(BF16), wout_ref[...], preferred_element_type=F32)
    o_ref[0] = x + gate * y


def _conv_mixer(x, mods, set_idx, g, w_in, conv_w, w_out, e_gate, e_up, e_down, tm):
    bsz, s, d = x.shape
    n_i = s // tm
    n_exp, _, f = e_gate.shape
    per_expert = (bsz * n_i) // n_exp
    sublane_pack = 2 * SUBLANES
    assert bsz * n_i == per_expert * n_exp
    assert d % (per_expert * sublane_pack) == 0 and f % (per_expert * sublane_pack) == 0
    w_slice = lambda b, i: ((b * n_i + i) // per_expert, (b * n_i + i) % per_expert, 0)
    up_spec = pl.BlockSpec((1, d // per_expert, f), w_slice)
    down_spec = pl.BlockSpec((1, f // per_expert, d), w_slice)
    return pl.pallas_call(
        _conv_mixer_kernel,
        out_shape=[jax.ShapeDtypeStruct(x.shape, F32), jax.ShapeDtypeStruct(e_gate.shape, BF16),
                   jax.ShapeDtypeStruct(e_up.shape, BF16), jax.ShapeDtypeStruct(e_down.shape, BF16)],
        grid=(bsz, n_i),
        in_specs=[
            pl.BlockSpec((1, tm, d), lambda b, i: (b, i, 0)),
            pl.BlockSpec((1, SUBLANES, 3 * d), lambda b, i: (set_idx, 0, 0)),
            pl.BlockSpec((1, d), lambda b, i: (0, 0)),
            pl.BlockSpec((d, 3 * d), lambda b, i: (0, 0)),
            pl.BlockSpec((CONV_WIDTH, d), lambda b, i: (0, 0)),
            pl.BlockSpec((d, d), lambda b, i: (0, 0)),
            up_spec, up_spec, down_spec,
        ],
        out_specs=[pl.BlockSpec((1, tm, d), lambda b, i: (b, i, 0)), up_spec, up_spec, down_spec],
        scratch_shapes=[pltpu.VMEM((SUBLANES, d), F32)],
        compiler_params=_cparams(("arbitrary", "arbitrary")),
        name="conv_mixer",
    )(x, mods, g, w_in, conv_w, w_out, e_gate, e_up, e_down)


def _ffn_kernel(x_ref, mod_ref, g_ref, wg_ref, wu_ref, wd_ref, o_ref):
    b = pl.program_id(0)
    d = x_ref.shape[2]
    x = x_ref[0]
    shift, scale, gate = _split_mod(mod_ref, b, d)
    h = _norm_mod(x, g_ref[...], scale, shift).astype(BF16)
    a = _silu(jnp.dot(h, wg_ref[...], preferred_element_type=F32)) * jnp.dot(
        h, wu_ref[...], preferred_element_type=F32)
    y = jnp.dot(a.astype(BF16), wd_ref[...], preferred_element_type=F32)
    o_ref[0] = x + gate * y


def _dense_ffn(x, mods, set_idx, g, w_gate, w_up, w_down, tm):
    bsz, s, d = x.shape
    f = w_gate.shape[1]
    resident = pl.Buffered(1)
    return pl.pallas_call(
        _ffn_kernel,
        out_shape=jax.ShapeDtypeStruct(x.shape, F32),
        grid=(bsz, s // tm),
        in_specs=[
            pl.BlockSpec((1, tm, d), lambda b, i: (b, i, 0)),
            pl.BlockSpec((1, SUBLANES, 3 * d), lambda b, i: (set_idx, 0, 0)),
            pl.BlockSpec((1, d), lambda b, i: (0, 0)),
            pl.BlockSpec((d, f), lambda b, i: (0, 0), pipeline_mode=resident),
            pl.BlockSpec((d, f), lambda b, i: (0, 0), pipeline_mode=resident),
            pl.BlockSpec((f, d), lambda b, i: (0, 0), pipeline_mode=resident),
        ],
        out_specs=pl.BlockSpec((1, tm, d), lambda b, i: (b, i, 0)),
        compiler_params=_cparams(("parallel", "parallel")),
        name="dense_ffn",
    )(x, mods, g, w_gate, w_up, w_down)


def _class_major(ref, tm, r_from, r_to):
    n_chunks = ref.shape[0]
    if r_to == r_from:
        return jnp.concatenate([ref[c] for c in range(n_chunks)], axis=1)

    def rows(res):
        start = (res % r_from) * (tm // r_from) + res // r_from
        return pl.ds(start, tm // r_to, stride=r_to // r_from)

    return jnp.concatenate(
        [jnp.concatenate([ref[c, rows(res), :] for c in range(n_chunks)], axis=1)
         for res in range(r_to)], axis=0)


def _split_matmul(a, b01):
    hi = a.astype(BF16)
    lo = (a - hi.astype(F32)).astype(BF16)
    return jnp.dot(hi, b01, preferred_element_type=F32) + jnp.dot(lo, b01, preferred_element_type=F32)


def _rotary_tables(pos, freq, tm):
    half = HEAD_DIM // 2
    per_row = LANES // half
    q4 = tm // per_row
    lane = lax.broadcasted_iota(jnp.int32, (1, LANES), 1)
    pos4 = pos[(per_row - 1) * q4:]
    for j in reversed(range(per_row - 1)):
        pos4 = jnp.where(lane // half == j, pos[j * q4:(j + 1) * q4], pos4)
    ang = pos4 * freq
    cos4, sin4 = jnp.cos(ang), jnp.sin(ang)
    li = lax.broadcasted_iota(jnp.int32, (LANES, LANES), 0)
    lj = lax.broadcasted_iota(jnp.int32, (LANES, LANES), 1)
    same_freq = li % half == lj % half
    sign = jnp.where(lj % HEAD_DIM < half, -1.0, 1.0)
    cos_rows, sin_rows = [], []
    for j in range(per_row):
        pick = same_freq & (li // half == j)
        cos_rows.append(_split_matmul(cos4, jnp.where(pick, 1.0, 0.0).astype(BF16)))
        sin_rows.append(_split_matmul(sin4, jnp.where(pick, sign, 0.0).astype(BF16)))
    return jnp.concatenate(cos_rows, axis=0), jnp.concatenate(sin_rows, axis=0)


def _store_lane_chunks(ref, val):
    for c in range(ref.shape[0]):
        ref[c] = val[:, c * LANES:(c + 1) * LANES]


def _rope(t, cos, sin_signed, first_half):
    out = []
    for c in range(t.shape[1] // LANES):
        tc = t[:, c * LANES:(c + 1) * LANES]
        rot = jnp.where(first_half, pltpu.roll(tc, LANES - HEAD_DIM // 2, 1),
                        pltpu.roll(tc, HEAD_DIM // 2, 1))
        out.append(tc * cos + rot * sin_signed)
    return jnp.concatenate(out, axis=1)


def _qkv_kernel(x_ref, mod_ref, g_ref, pos_ref, freq_ref, w_ref, o0_ref, o1_ref, o2_ref,
                scr_a, scr_b):
    b = pl.program_id(0)
    tm, d = x_ref.shape[1], x_ref.shape[2]
    shift, scale, _ = _split_mod(mod_ref, b, d)
    first_half = (lax.broadcasted_iota(jnp.int32, (1, LANES), 1) % HEAD_DIM) < HEAD_DIM // 2
    cos_t, sin_t = _rotary_tables(pos_ref[0].astype(F32), freq_ref[...], tm)
    _store_lane_chunks(scr_a, jnp.concatenate(
        [_norm_mod(x_ref[0], g_ref[...], scale, shift), cos_t, sin_t], axis=1))
    scrs, cur, cur_r = (scr_a, scr_b), 0, 1
    gl = GROUP_LANES
    for gi, ((_, r), o_ref) in enumerate(zip(ATTN_GROUPS, (o0_ref, o1_ref, o2_ref))):
        vals = _class_major(scrs[cur], tm, cur_r, r)
        if r != cur_r and gi + 1 < len(ATTN_GROUPS):
            cur, cur_r = 1 - cur, r
            _store_lane_chunks(scrs[cur], vals)
        hg, cos, sin_signed = vals[:, :d].astype(BF16), vals[:, d:d + LANES], vals[:, d + LANES:]
        qkv = jnp.dot(hg, w_ref[gi], preferred_element_type=F32)
        q = _rope(qkv[:, :gl], cos, sin_signed, first_half) * (HEAD_DIM ** -0.5)
        k = _rope(qkv[:, gl:2 * gl], cos, sin_signed, first_half)
        o_ref[0, :, :gl] = q.astype(BF16)
        o_ref[0, :, gl:2 * gl] = k.astype(BF16)
        o_ref[0, :, 2 * gl:] = qkv[:, 2 * gl:].astype(BF16)


def _qkv_rope(x, mods, set_idx, g, pos3, freq, w_groups, tm):
    bsz, s, d = x.shape
    n_groups = len(ATTN_GROUPS)
    out = jax.ShapeDtypeStruct((bsz, s, 3 * GROUP_LANES), BF16)
    return pl.pallas_call(
        _qkv_kernel,
        out_shape=[out] * n_groups,
        grid=(bsz, s // tm),
        in_specs=[
            pl.BlockSpec((1, tm, d), lambda b, i: (b, i, 0)),
            pl.BlockSpec((1, SUBLANES, 3 * d), lambda b, i: (set_idx, 0, 0)),
            pl.BlockSpec((1, d), lambda b, i: (0, 0)),
            pl.BlockSpec((1, tm, 1), lambda b, i: (b, i, 0)),
            pl.BlockSpec((1, LANES), lambda b, i: (0, 0)),
            pl.BlockSpec((n_groups, d, 3 * GROUP_LANES), lambda b, i: (0, 0, 0)),
        ],
        out_specs=[pl.BlockSpec((1, tm, 3 * GROUP_LANES), lambda b, i: (b, i, 0))] * n_groups,
        scratch_shapes=[pltpu.VMEM((d // LANES + 2, tm, LANES), F32)] * 2,
        compiler_params=_cparams(("parallel", "parallel")),
        name="qkv_rope",
    )(x, mods, g, pos3, freq, w_groups)


def _store_rows(ref, a, val):
    st, per_tile = ATTN_STEPS, ref.shape[1]
    if per_tile >= st:
        ref[(a * st) // per_tile, pl.ds((a * st) % per_tile, st), :] = val
    else:
        for j in range(st // per_tile):
            ref[a * (st // per_tile) + j] = val[j * per_tile:(j + 1) * per_tile]


def _attn_kernel(q_ref, kc_ref, vc_ref, kp_ref, vp_ref, o_ref, lse_ref):
    i = pl.program_id(2)
    st = ATTN_STEPS
    tq = q_ref.shape[0] * q_ref.shape[1]
    q = q_ref[...].reshape(tq, GROUP_LANES)
    k_rows = jnp.concatenate([kp_ref[...].reshape(st, GROUP_LANES),
                              kc_ref[...].reshape(tq, GROUP_LANES)], axis=0)
    v_rows = jnp.concatenate([vp_ref[...].reshape(st, GROUP_LANES),
                              vc_ref[...].reshape(tq, GROUP_LANES)], axis=0)
    lane = lax.broadcasted_iota(jnp.int32, (1, LANES), 1)
    qi = lax.broadcasted_iota(jnp.int32, (st, 2 * st), 0)
    kj = lax.broadcasted_iota(jnp.int32, (st, 2 * st), 1)
    band = (kj >= qi) & (kj <= qi + st)
    band_first = band & ((kj >= st) | (i > 0))
    nt_dims = (((1,), (1,)), ((), ()))
    zero = jnp.zeros((), BF16)

    k_heads, v_heads = [], []
    for pair in range(PAIRS_PER_GROUP):
        sl = slice(pair * LANES, (pair + 1) * LANES)
        kp_, vp_ = k_rows[:, sl], v_rows[:, sl]
        if HEADS_PER_GROUP - 2 * pair >= 2:
            halves = [lane < HEAD_DIM, lane >= HEAD_DIM]
            k_heads.append([jnp.where(m, kp_, zero) for m in halves])
            v_heads.append([jnp.where(m, vp_, zero) for m in halves])
        else:
            k_heads.append([kp_])
            v_heads.append([vp_])

    blocks = [(a, pair, hh) for a in range(tq // st) for pair in range(PAIRS_PER_GROUP)
              for hh in range(len(k_heads[pair]))]
    scores = []
    for a, pair, hh in blocks:
        qa = q[a * st:(a + 1) * st, pair * LANES:(pair + 1) * LANES]
        s = lax.dot_general(qa, k_heads[pair][hh][a * st:(a + 2) * st], nt_dims,
                            preferred_element_type=F32)
        scores.append(jnp.where(band if a else band_first, s, NEG_INF))
    s_all = jnp.concatenate(scores, axis=0)
    m_all = jnp.max(s_all, axis=-1, keepdims=True)
    p_all = jnp.exp(s_all - m_all)
    l_all = jnp.sum(p_all, axis=-1, keepdims=True)
    p_all = p_all.astype(BF16)
    inv_all = 1.0 / l_all
    lse_all = m_all + jnp.log(l_all)

    for a in range(tq // st):
        lse_tile = jnp.zeros((st, LANES), F32)
        o_tiles = []
        for pair in range(PAIRS_PER_GROUP):
            acc = jnp.zeros((st, LANES), F32)
            for hh in range(len(k_heads[pair])):
                rows = slice(blocks.index((a, pair, hh)) * st, (blocks.index((a, pair, hh)) + 1) * st)
                pv = jnp.dot(p_all[rows], v_heads[pair][hh][a * st:(a + 2) * st],
                             preferred_element_type=F32)
                acc = acc + pv * inv_all[rows]
                lse_tile = jnp.where(lane == 2 * pair + hh, lse_all[rows], lse_tile)
            o_tiles.append(acc.astype(BF16))
        _store_rows(o_ref, a, jnp.concatenate(o_tiles, axis=1))
        _store_rows(lse_ref, a, lse_tile)


def _dilated_attn(qkv, r, tmp):
    bsz, s, width = qkv.shape
    gl, st = GROUP_LANES, ATTN_STEPS
    cls_rows = tmp // r
    tq = min(512, s // r)
    nt = tq // cls_rows
    n_tiles = s // tmp
    n_q = (s // r) // tq
    sub = tq // st
    main = qkv.reshape(bsz, n_tiles, r, cls_rows, width)

    def main_spec(col):
        return pl.BlockSpec((None, nt, None, cls_rows, gl), lambda b, res, i: (b, i, res, 0, col))

    if cls_rows >= st:
        per_tile = cls_rows // st
        prev_arr = qkv.reshape(bsz, n_tiles, r, per_tile, st, width)

        def prev_spec(col):
            def index(b, res, i):
                c = jnp.maximum(i * sub - 1, 0)
                return (b, c // per_tile, res, c % per_tile, 0, col)
            return pl.BlockSpec((None, None, None, None, st, gl), index)
    else:
        ntp = st // cls_rows
        prev_arr = qkv.reshape(bsz, n_tiles // ntp, ntp, r, cls_rows, width)

        def prev_spec(col):
            return pl.BlockSpec((None, None, ntp, None, cls_rows, gl),
                                lambda b, res, i: (b, jnp.maximum(i * sub - 1, 0), 0, res, 0, col))

    o5 = jax.ShapeDtypeStruct((bsz, n_tiles, r, cls_rows, gl), BF16)
    l5 = jax.ShapeDtypeStruct((bsz, n_tiles, r, cls_rows, LANES), F32)

    o, lse = pl.pallas_call(
        _attn_kernel,
        out_shape=[o5, l5],
        grid=(bsz, r, n_q),
        in_specs=[main_spec(0), main_spec(1), main_spec(2), prev_spec(1), prev_spec(2)],
        out_specs=[
            pl.BlockSpec((None, nt, None, cls_rows, gl), lambda b, res, i: (b, i, res, 0, 0)),
            pl.BlockSpec((None, nt, None, cls_rows, LANES), lambda b, res, i: (b, i, res, 0, 0)),
        ],
        compiler_params=_cparams(("parallel", "parallel", "parallel")),
        name=f"dilated_attn_r{r}",
    )(main, main, main, prev_arr, prev_arr)
    return o.reshape(bsz, s, gl), lse.reshape(bsz, s, LANES)


def _token_major(scr, val, tm, r):
    if r == 1:
        return val
    n = tm // r
    for c in range(scr.shape[0]):
        for res in range(r):
            scr[c, pl.ds(res, n, stride=r), :] = val[res * n:(res + 1) * n, c * LANES:(c + 1) * LANES]
    return jnp.concatenate([scr[c] for c in range(scr.shape[0])], axis=1)


def _attn_out_route_kernel(x_ref, mod_a_ref, mod_m_ref, g_ref, o0_ref, o1_ref, o2_ref,
                           l0_ref, l1_ref, l2_ref, expand_ref, wo_ref, wr_ref,
                           x_out_ref, h_out_ref, route_ref, wts_ref, cnt_ref,
                           o_scr, l_scr, base_scr):
    b, i = pl.program_id(0), pl.program_id(1)
    tm, d = x_ref.shape[1], x_ref.shape[2]
    n_exp = cnt_ref.shape[0]

    @pl.when((b == 0) & (i == 0))
    def _():
        base_scr[...] = jnp.zeros_like(base_scr)

    o_refs, l_refs = (o0_ref, o1_ref, o2_ref), (l0_ref, l1_ref, l2_ref)
    outs, lses = [], []
    for gi, (_, r) in enumerate(ATTN_GROUPS):
        lses.append(_token_major(l_scr.at[gi], l_refs[gi][0], tm, r))
        outs.append(_token_major(o_scr.at[gi], o_refs[gi][0].astype(F32), tm, r))
    _, _, gate_a = _split_mod(mod_a_ref, b, d)
    shift, scale, _ = _split_mod(mod_m_ref, b, d)
    lane = lax.broadcasted_iota(jnp.int32, (1, LANES), 1)
    n_slab = d // LANES
    rc = tm // ROUTE_ROW_CHUNKS
    ri = lax.broadcasted_iota(jnp.int32, (rc, rc), 0)
    ci = lax.broadcasted_iota(jnp.int32, (rc, rc), 1)
    before = jnp.where(ci < ri, 1.0, 0.0).astype(BF16)
    count = base_scr[0:1, :]

    for c in range(ROUTE_ROW_CHUNKS):
        rows = slice(c * rc, (c + 1) * rc)
        ls = [l[rows] for l in lses]
        m = jnp.maximum(jnp.maximum(ls[0], ls[1]), ls[2])
        es = [jnp.exp(l - m) for l in ls]
        inv = 1.0 / (es[0] + es[1] + es[2])
        scaled = []
        for gi in range(len(ATTN_GROUPS)):
            alpha = es[gi] * inv
            hi = alpha.astype(BF16)
            lo = (alpha - hi.astype(F32)).astype(BF16)
            wide = jnp.dot(hi, expand_ref[...], preferred_element_type=F32) + jnp.dot(
                lo, expand_ref[...], preferred_element_type=F32)
            scaled.append((outs[gi][rows] * wide).astype(BF16))
        attn = jnp.dot(jnp.concatenate(scaled, axis=1), wo_ref[...], preferred_element_type=F32)
        x = x_ref[0, rows, :] + gate_a * attn
        x_out_ref[0, rows, :] = x

        h = _norm_mod(x, g_ref[...], scale, shift)
        for k in range(n_slab):
            h_out_ref[pl.ds(c * rc * n_slab + k, rc, stride=n_slab), :] = h[:, k * LANES:(k + 1) * LANES]

        h_hi = h.astype(BF16)
        h_lo = (h - h_hi.astype(F32)).astype(BF16)
        logits = (jnp.dot(h_hi, wr_ref[0], preferred_element_type=F32)
                  + jnp.dot(h_lo, wr_ref[0], preferred_element_type=F32)
                  + jnp.dot(h_hi, wr_ref[1], preferred_element_type=F32))
        logits = jnp.where(lane < n_exp, logits, NEG_INF)
        m1 = jnp.max(logits, axis=-1, keepdims=True)
        i1 = jnp.min(jnp.where(logits == m1, lane, LANES), axis=-1, keepdims=True)
        rest = jnp.where(lane == i1, NEG_INF, logits)
        m2 = jnp.max(rest, axis=-1, keepdims=True)
        i2 = jnp.min(jnp.where(rest == m2, lane, LANES), axis=-1, keepdims=True)
        e2 = jnp.exp(m2 - m1)
        w1 = 1.0 / (1.0 + e2)
        w2 = e2 * w1
        wts_ref[rows, :] = jnp.where(lane == 0, w1, jnp.where(lane == 1, w2, 0.0))

        oh1, oh2 = (lane == i1), (lane == i2)
        picks = jnp.where(oh1 | oh2, 1.0, 0.0).astype(BF16)
        rank = jnp.dot(before, picks, preferred_element_type=F32) + count
        r1 = jnp.sum(jnp.where(oh1, rank, 0.0), axis=-1, keepdims=True).astype(jnp.int32)
        r2 = jnp.sum(jnp.where(oh2, rank, 0.0), axis=-1, keepdims=True).astype(jnp.int32)
        route = jnp.where(lane == 0, r1, jnp.where(lane == 1, r2, jnp.where(
            lane == 2, i1, jnp.where(lane == 3, i2, 0))))
        route_ref[:, rows] = jnp.transpose(route)[:SUBLANES, :]
        count = count + jnp.sum(picks.astype(F32), axis=0, keepdims=True)

    base_scr[...] = jnp.broadcast_to(count, base_scr.shape)
    cnt_ref[...] = jnp.broadcast_to(count, cnt_ref.shape).astype(jnp.int32)


def _attn_out_route(x, mods, set_a, set_m, g, outs, lses, expand, w_o, w_router, tm):
    bsz, s, d = x.shape
    t = bsz * s
    n_i = s // tm
    n_groups = len(ATTN_GROUPS)
    gl = GROUP_LANES
    tok = lambda b, i: (b, i, 0)
    flat = lambda b, i: (b * n_i + i, 0)
    const2 = lambda b, i: (0, 0)
    return pl.pallas_call(
        _attn_out_route_kernel,
        out_shape=[
            jax.ShapeDtypeStruct((bsz, s, d), F32),
            jax.ShapeDtypeStruct((t * ROW_SLAB, LANES), F32),
            jax.ShapeDtypeStruct((SUBLANES, t), jnp.int32),
            jax.ShapeDtypeStruct((t, LANES), F32),
            jax.ShapeDtypeStruct((SUBLANES, LANES), jnp.int32),
        ],
        grid=(bsz, n_i),
        in_specs=[
            pl.BlockSpec((1, tm, d), tok),
            pl.BlockSpec((1, SUBLANES, 3 * d), lambda b, i: (set_a, 0, 0)),
            pl.BlockSpec((1, SUBLANES, 3 * d), lambda b, i: (set_m, 0, 0)),
            pl.BlockSpec((1, d), const2),
        ] + [pl.BlockSpec((1, tm, gl), tok)] * n_groups
          + [pl.BlockSpec((1, tm, LANES), tok)] * n_groups + [
            pl.BlockSpec((LANES, gl), const2),
            pl.BlockSpec((n_groups * gl, d), const2),
            pl.BlockSpec((2, d, LANES), lambda b, i: (0, 0, 0)),
        ],
        out_specs=[
            pl.BlockSpec((1, tm, d), tok),
            pl.BlockSpec((tm * ROW_SLAB, LANES), flat),
            pl.BlockSpec((SUBLANES, tm), lambda b, i: (0, b * n_i + i)),
            pl.BlockSpec((tm, LANES), flat),
            pl.BlockSpec((SUBLANES, LANES), const2),
        ],
        scratch_shapes=[pltpu.VMEM((n_groups, gl // LANES, tm, LANES), F32),
                        pltpu.VMEM((n_groups, 1, tm, LANES), F32),
                        pltpu.VMEM((SUBLANES, LANES), F32)],
        compiler_params=_cparams(("arbitrary", "arbitrary")),
        name="attn_out_route",
    )(x, mods, mods, g, *outs, *lses, expand, w_o, w_router)


DMA_LOOP_UNROLL = 16


def _for_each(n, fn):
    def body(g, carry):
        for u in range(DMA_LOOP_UNROLL):
            fn(g * DMA_LOOP_UNROLL + u)
        return carry
    lax.fori_loop(0, n // DMA_LOOP_UNROLL, body, 0)


def _row_copy(src, src_row, dst, dst_row, sem):
    return pltpu.make_async_copy(src.at[pl.ds(pl.multiple_of(src_row * ROW_SLAB, ROW_SLAB), ROW_SLAB)],
                                 dst.at[pl.ds(pl.multiple_of(dst_row * ROW_SLAB, ROW_SLAB), ROW_SLAB)],
                                 sem)


def _dispatch_kernel(fill_ref, pos_ref, h_ref, xs_hbm, zeros_scr, sem, zsem, *, tmd, te, n_exp,
                     n_tiles):
    step = pl.program_id(0)

    def zero_fills():
        def zero_copy(first_row, n):
            start = pl.multiple_of(first_row * ROW_SLAB, ROW_SLAB)
            return pltpu.make_async_copy(zeros_scr.at[pl.ds(0, n * ROW_SLAB)],
                                         xs_hbm.at[pl.ds(start, n * ROW_SLAB)], zsem)
        fills = []
        for e in range(n_exp):
            row, length = fill_ref[e], fill_ref[n_exp + e]
            for bit in reversed(range(te.bit_length() - 1)):
                take = (length & (1 << bit)) != 0
                fills.append((take, zero_copy(row, 1 << bit)))
                row = row + jnp.where(take, 1 << bit, 0)
        n_used = fill_ref[2 * n_exp]
        for k in range(n_exp):
            fills.append((n_used + k < n_tiles, zero_copy((n_used + k) * te, te)))
        return fills

    @pl.when(step == 0)
    def _():
        zeros_scr[...] = jnp.zeros_like(zeros_scr)
        for take, cp in zero_fills():
            pl.when(take)(cp.start)
        for take, cp in zero_fills():
            pl.when(take)(cp.wait)

    def copies(tok):
        return [_row_copy(h_ref, tok, xs_hbm, pos_ref[0, 0, k * tmd + tok], sem)
                for k in range(TOP_K)]

    _for_each(tmd, lambda tok: [cp.start(priority=k) for k, cp in enumerate(copies(tok))])
    _for_each(tmd, lambda tok: [cp.wait() for cp in copies(tok)])


def _moe_dispatch(fill, pos_blocks, h_rows, n_tiles, n_exp, tmd, te):
    n_steps = pos_blocks.shape[0]
    assert te & (te - 1) == 0
    kernel = functools.partial(_dispatch_kernel, tmd=tmd, te=te, n_exp=n_exp, n_tiles=n_tiles)
    return pl.pallas_call(
        kernel,
        out_shape=jax.ShapeDtypeStruct((n_tiles * te * ROW_SLAB, LANES), F32),
        grid_spec=pltpu.PrefetchScalarGridSpec(
            num_scalar_prefetch=1,
            grid=(n_steps,),
            in_specs=[
                pl.BlockSpec((1, 1, tmd * TOP_K), lambda s, cnt: (s, 0, 0), memory_space=pltpu.SMEM),
                pl.BlockSpec((tmd * ROW_SLAB, LANES), lambda s, cnt: (s, 0)),
            ],
            out_specs=pl.BlockSpec(memory_space=pl.ANY),
            scratch_shapes=[pltpu.VMEM((te * ROW_SLAB, LANES), F32), pltpu.SemaphoreType.DMA,
                            pltpu.SemaphoreType.DMA],
        ),
        compiler_params=_cparams(("arbitrary",)),
        name="moe_dispatch",
    )(fill, pos_blocks, h_rows)


def _slab_rows_to_matrix(ref, n_rows, n_chunks):
    return jnp.concatenate([ref[pl.ds(c, n_rows, stride=n_chunks), :] for c in range(n_chunks)],
                           axis=1)


def _experts_kernel(tile_expert_ref, n_used_ref, x_ref, wg_ref, wu_ref, wd_ref, y_ref, *, te,
                    f_chunks):
    del tile_expert_ref
    used = pl.program_id(0) < n_used_ref[0]

    @pl.when(jnp.logical_not(used))
    def _():
        y_ref[...] = jnp.zeros_like(y_ref)

    @pl.when(used)
    def _():
        d = wg_ref.shape[1]
        n_chunks = d // LANES
        x = _slab_rows_to_matrix(x_ref, te, n_chunks).astype(BF16)
        y = jnp.zeros((te, d), F32)
        for lo, hi in f_chunks:
            gate = jnp.dot(x, wg_ref[0, :, lo:hi], preferred_element_type=F32)
            up = jnp.dot(x, wu_ref[0, :, lo:hi], preferred_element_type=F32)
            y = y + jnp.dot((_silu(gate) * up).astype(BF16), wd_ref[0, lo:hi, :],
                            preferred_element_type=F32)
        for c in range(n_chunks):
            y_ref[pl.ds(c, te, stride=n_chunks), :] = y[:, c * LANES:(c + 1) * LANES]


def _f_chunks(f):
    if f <= 1536:
        return ((0, f),)
    half = (f // 2 + 255) // 256 * 256
    return ((0, half), (half, f))


def _moe_experts(tile_expert, n_used, xs, w_gate, w_up, w_down, te):
    n_tiles = tile_expert.shape[0]
    _, d, f = w_gate.shape
    kernel = functools.partial(_experts_kernel, te=te, f_chunks=_f_chunks(f))
    weights = lambda i, te_, nu_: (te_[i], 0, 0)
    return pl.pallas_call(
        kernel,
        out_shape=jax.ShapeDtypeStruct(xs.shape, F32),
        grid_spec=pltpu.PrefetchScalarGridSpec(
            num_scalar_prefetch=2,
            grid=(n_tiles,),
            in_specs=[
                pl.BlockSpec((te * ROW_SLAB, LANES), lambda i, te_, nu_: (jnp.minimum(i, nu_[0] - 1), 0)),
                pl.BlockSpec((1, d, f), weights),
                pl.BlockSpec((1, d, f), weights),
                pl.BlockSpec((1, f, d), weights),
            ],
            out_specs=pl.BlockSpec((te * ROW_SLAB, LANES), lambda i, te_, nu_: (i, 0)),
        ),
        compiler_params=_cparams(("arbitrary",)),
        name="moe_experts",
    )(tile_expert, n_used, xs, w_gate, w_up, w_down)


def _combine_kernel(pos_ref, pos_next_ref, x_ref, mod_ref, g_ref, wts_ref, ys_hbm, o_ref, buf, sems,
                    *, tmc):
    b = pl.program_id(0)
    step = b * pl.num_programs(1) + pl.program_id(1)
    n_steps = pl.num_programs(0) * pl.num_programs(1)
    slot = step % 2
    d = x_ref.shape[2]
    n_chunks = d // LANES

    def copies(p_ref, slot_, tok):
        return [_row_copy(ys_hbm, p_ref[0, 0, k * tmc + tok], buf.at[slot_], k * tmc + tok,
                          sems.at[slot_]) for k in range(TOP_K)]

    def fetch(p_ref, slot_):
        _for_each(tmc, lambda tok: [cp.start(priority=k)
                                    for k, cp in enumerate(copies(p_ref, slot_, tok))])

    pl.when(step == 0)(lambda: fetch(pos_ref, slot))
    pl.when(step + 1 < n_steps)(lambda: fetch(pos_next_ref, 1 - slot))
    _for_each(tmc, lambda tok: [cp.wait() for cp in copies(pos_ref, slot, tok)])

    wts = wts_ref[...]
    y = jnp.zeros((tmc, d), F32)
    rows = buf.at[slot]
    for k in range(TOP_K):
        yk = _slab_rows_to_matrix(rows.at[pl.ds(k * tmc * ROW_SLAB, tmc * ROW_SLAB)], tmc, n_chunks)
        y = y + wts[:, k:k + 1] * yk
    _, _, gate = _split_mod(mod_ref, b, d)
    x = x_ref[0] + gate * y
    o_ref[0] = x * lax.rsqrt(jnp.mean(x * x, axis=-1, keepdims=True) + RMS_EPS) * g_ref[...]


def _moe_combine(pos_blocks, x, mods, set_idx, final_g, wts, ys, tmc):
    bsz, s, d = x.shape
    n_i = s // tmc
    last = bsz * n_i - 1
    kernel = functools.partial(_combine_kernel, tmc=tmc)
    return pl.pallas_call(
        kernel,
        out_shape=jax.ShapeDtypeStruct(x.shape, F32),
        grid=(bsz, n_i),
        in_specs=[
            pl.BlockSpec((1, 1, tmc * TOP_K), lambda b, i: (b * n_i + i, 0, 0),
                         memory_space=pltpu.SMEM),
            pl.BlockSpec((1, 1, tmc * TOP_K), lambda b, i: (jnp.minimum(b * n_i + i + 1, last), 0, 0),
                         memory_space=pltpu.SMEM),
            pl.BlockSpec((1, tmc, d), lambda b, i: (b, i, 0)),
            pl.BlockSpec((1, SUBLANES, 3 * d), lambda b, i: (set_idx, 0, 0)),
            pl.BlockSpec((1, d), lambda b, i: (0, 0)),
            pl.BlockSpec((tmc, LANES), lambda b, i: (b * n_i + i, 0)),
            pl.BlockSpec(memory_space=pl.ANY),
        ],
        out_specs=pl.BlockSpec((1, tmc, d), lambda b, i: (b, i, 0)),
        scratch_shapes=[pltpu.VMEM((2, TOP_K * tmc * ROW_SLAB, LANES), F32),
                        pltpu.SemaphoreType.DMA((2,))],
        compiler_params=_cparams(("arbitrary", "arbitrary")),
        name="moe_combine",
    )(pos_blocks, pos_blocks, x, mods, final_g, wts, ys)


def _pad_group_columns(w, which):
    width = HEADS_PER_GROUP * HEAD_DIM
    n_groups = len(ATTN_GROUPS)
    cols = w[:, which * n_groups * width:(which + 1) * n_groups * width]
    cols = cols.reshape(w.shape[0], n_groups, width)
    return jnp.pad(cols, ((0, 0), (0, 0), (0, GROUP_LANES - width)))


def kernel(x, c, positions, mod_w, mod_b, norm_g, conv_w_in, conv_w, conv_w_out, ffn_w_gate,
           ffn_w_up, ffn_w_down, attn_w_qkv, attn_w_o, router_w, moe_w_gate, moe_w_up, moe_w_down,
           final_g):
    bsz, s, d = x.shape
    t = bsz * s
    n_groups = len(ATTN_GROUPS)
    n_exp = router_w.shape[-1]
    assert all(w // r == ATTN_STEPS for w, r in ATTN_GROUPS)
    assert bsz <= SUBLANES and n_exp <= SUBLANES and d == ROW_SLAB * LANES
    tm = min(512, s)
    assert s % tm == 0 and all(s % (r * ATTN_STEPS) == 0 and tm % r == 0 for _, r in ATTN_GROUPS)

    c_pad = jnp.pad(c.astype(F32), ((0, SUBLANES - bsz), (0, 0)))
    mods = _ada_params(c_pad, mod_w.reshape(-1, d, 3 * d), mod_b.reshape(-1, 1, 3 * d))

    x, e_gate, e_up, e_down = _conv_mixer(
        x, mods, 0, norm_g[0, 0][None], conv_w_in[0].astype(BF16), conv_w[0],
        conv_w_out[0].astype(BF16), moe_w_gate[0], moe_w_up[0], moe_w_down[0], tm)
    x = _dense_ffn(x, mods, 1, norm_g[0, 1][None], ffn_w_gate[0].astype(BF16),
                   ffn_w_up[0].astype(BF16), ffn_w_down[0].astype(BF16), tm)

    w_qkv = attn_w_qkv[0]
    w_groups = jnp.concatenate([_pad_group_columns(w_qkv, which) for which in range(3)], axis=2)
    w_groups = w_groups.transpose(1, 0, 2).astype(BF16)
    inv_freq = ROPE_THETA ** (-jnp.arange(0, HEAD_DIM, 2, dtype=F32) / HEAD_DIM)
    freq = jnp.tile(inv_freq, LANES // (HEAD_DIM // 2))[None]
    qkvs = _qkv_rope(x, mods, 2, norm_g[1, 0][None], positions[..., None], freq, w_groups, tm)
    outs, lses = zip(*[_dilated_attn(qkv, r, tm) for qkv, (_, r) in zip(qkvs, ATTN_GROUPS)])

    width = HEADS_PER_GROUP * HEAD_DIM
    w_o = jnp.pad(attn_w_o[0].reshape(n_groups, width, d), ((0, 0), (0, GROUP_LANES - width), (0, 0)))
    w_o = w_o.reshape(n_groups * GROUP_LANES, d).astype(BF16)
    expand = (jnp.arange(GROUP_LANES)[None, :] // HEAD_DIM == jnp.arange(LANES)[:, None]).astype(BF16)
    w_router = jnp.pad(router_w[0], ((0, 0), (0, LANES - n_exp)))
    w_router_hi = w_router.astype(BF16)
    w_router = jnp.stack([w_router_hi, (w_router - w_router_hi.astype(F32)).astype(BF16)])

    x, h_rows, route, wts, counts = _attn_out_route(
        x, mods, 2, 3, norm_g[1, 1][None], outs, lses, expand, w_o, w_router, tm)

    te = min(512, t)
    n_tiles = (t * TOP_K) // te + n_exp
    counts = counts[0, :n_exp]
    tiles_per_expert = (counts + te - 1) // te
    ends = jnp.cumsum(tiles_per_expert)
    first_row = (ends - tiles_per_expert) * te
    n_used = ends[-1:]
    experts = route[TOP_K:2 * TOP_K]
    pos = route[:TOP_K] + sum(jnp.where(experts == e, first_row[e], 0) for e in range(n_exp))
    fill = jnp.concatenate([first_row + counts, tiles_per_expert * te - counts, n_used]).astype(jnp.int32)
    idx = jnp.minimum(jnp.arange(n_tiles, dtype=jnp.int32), n_used - 1)
    tile_expert = jnp.sum(idx[:, None] >= ends[None, :], axis=1).astype(jnp.int32)

    pos_blocks = pos.reshape(TOP_K, t // tm, tm).transpose(1, 0, 2).reshape(t // tm, 1, TOP_K * tm)
    xs = _moe_dispatch(fill, pos_blocks, h_rows, n_tiles, n_exp, tm, te)
    ys = _moe_experts(tile_expert, n_used.astype(jnp.int32), xs, e_gate, e_up, e_down, te)
    return _moe_combine(pos_blocks, x, mods, 3, final_g[None], wts, ys, tm)
```

```python
import functools
import math

import jax
import jax.numpy as jnp
from jax import lax
from jax.experimental import pallas as pl
from jax.experimental.pallas import tpu as pltpu

CONV_WIDTH = 3
ATTN_GROUPS = ((128, 1), (512, 4), (2048, 16))
HEADS_PER_GROUP = 5
HEAD_DIM = 64
ROPE_THETA = 10000.0
TOP_K = 2
RMS_EPS = 1e-6

LANES = 128
SUBLANES = 8
VMEM_LIMIT_BYTES = 56 * 1024 * 1024

GROUP_LANES = 384
PAIRS_PER_GROUP = GROUP_LANES // LANES
ATTN_STEPS = 128
ROW_SLAB = 8
ROUTE_ROW_CHUNKS = 2

F32 = jnp.float32
BF16 = jnp.bfloat16
NEG_INF = float("-inf")


def _cparams(semantics):
    return pltpu.CompilerParams(dimension_semantics=semantics, vmem_limit_bytes=VMEM_LIMIT_BYTES)


def _norm_mod(x, g, scale, shift):
    y = x * lax.rsqrt(jnp.mean(x * x, axis=-1, keepdims=True) + RMS_EPS)
    return (y * g) * (1.0 + scale) + shift


def _split_mod(mod_ref, b, d):
    row = mod_ref[0, pl.ds(b, 1), :]
    return row[:, :d], row[:, d:2 * d], row[:, 2 * d:]


def _silu(x):
    return x * jax.nn.sigmoid(x)


def _ada_kernel(c_ref, w_ref, b_ref, o_ref):
    s, w = _silu(c_ref[...]), w_ref[0]
    s_hi, w_hi = s.astype(BF16), w.astype(BF16)
    s_lo = (s - s_hi.astype(F32)).astype(BF16)
    w_lo = (w - w_hi.astype(F32)).astype(BF16)
    both = jnp.dot(jnp.concatenate([s_hi, s_lo], axis=0), w_hi, preferred_element_type=F32)
    o_ref[0] = (both[:SUBLANES] + both[SUBLANES:]
                + jnp.dot(s_hi, w_lo, preferred_element_type=F32) + b_ref[0])


def _ada_params(c_pad, mod_w, mod_b):
    n_sets, d, d3 = mod_w.shape
    tn = d
    return pl.pallas_call(
        _ada_kernel,
        out_shape=jax.ShapeDtypeStruct((n_sets, SUBLANES, d3), F32),
        grid=(n_sets, d3 // tn),
        in_specs=[
            pl.BlockSpec((SUBLANES, d), lambda s, j: (0, 0)),
            pl.BlockSpec((1, d, tn), lambda s, j: (s, 0, j)),
            pl.BlockSpec((1, 1, tn), lambda s, j: (s, 0, j)),
        ],
        out_specs=pl.BlockSpec((1, SUBLANES, tn), lambda s, j: (s, 0, j)),
        compiler_params=_cparams(("parallel", "parallel")),
        name="ada_params",
    )(c_pad, mod_w, mod_b)


def _conv_mixer_kernel(x_ref, mod_ref, g_ref, win_ref, cw_ref, wout_ref, eg_ref, eu_ref, ed_ref,
                       o_ref, eg_out, eu_out, ed_out, carry_ref):
    b, i = pl.program_id(0), pl.program_id(1)
    tm, d = x_ref.shape[1], x_ref.shape[2]
    for src, dst in ((eg_ref, eg_out), (eu_ref, eu_out), (ed_ref, ed_out)):
        dst[...] = src[...].astype(BF16)

    @pl.when(i == 0)
    def _():
        carry_ref[...] = jnp.zeros_like(carry_ref)

    x = x_ref[0]
    shift, scale, gate = _split_mod(mod_ref, b, d)
    h = _norm_mod(x, g_ref[...], scale, shift).astype(BF16)
    bcu = jnp.dot(h, win_ref[...], preferred_element_type=F32)
    b_gate, v = bcu[:, :d], bcu[:, d:2 * d] * bcu[:, 2 * d:]
    row = lax.broadcasted_iota(jnp.int32, (tm, 1), 0)
    prev = carry_ref[...]
    v1 = jnp.where(row == 0, prev[7:8], pltpu.roll(v, 1, 0))
    v2 = jnp.where(row == 0, prev[6:7], jnp.where(row == 1, prev[7:8], pltpu.roll(v, 2, 0)))
    cw = cw_ref[...]
    conv = cw[0:1] * v2 + cw[1:2] * v1 + cw[2:3] * v
    carry_ref[...] = v[tm - SUBLANES:, :]
    y = jnp.dot((b_gate * conv).astype(BF16), wout_ref[...], preferred_element_type=F32)
    o_ref[0] = x + gate * y


def _conv_mixer(x, mods, set_idx, g, w_in, conv_w, w_out, e_gate, e_up, e_down, tm):
    bsz, s, d = x.shape
    n_i = s // tm
    n_exp, _, f = e_gate.shape
    per_expert = (bsz * n_i) // n_exp
    sublane_pack = 2 * SUBLANES
    assert bsz * n_i == per_expert * n_exp
    assert d % (per_expert * sublane_pack) == 0 and f % (per_expert * sublane_pack) == 0
    w_slice = lambda b, i: ((b * n_i + i) // per_expert, (b * n_i + i) % per_expert, 0)
    up_spec = pl.BlockSpec((1, d // per_expert, f), w_slice)
    down_spec = pl.BlockSpec((1, f // per_expert, d), w_slice)
    return pl.pallas_call(
        _conv_mixer_kernel,
        out_shape=[jax.ShapeDtypeStruct(x.shape, F32), jax.ShapeDtypeStruct(e_gate.shape, BF16),
                   jax.ShapeDtypeStruct(e_up.shape, BF16), jax.ShapeDtypeStruct(e_down.shape, BF16)],
        grid=(bsz, n_i),
        in_specs=[
            pl.BlockSpec((1, tm, d), lambda b, i: (b, i, 0)),
            pl.BlockSpec((1, SUBLANES, 3 * d), lambda b, i: (set_idx, 0, 0)),
            pl.BlockSpec((1, d), lambda b, i: (0, 0)),
            pl.BlockSpec((d, 3 * d), lambda b, i: (0, 0)),
            pl.BlockSpec((CONV_WIDTH, d), lambda b, i: (0, 0)),
            pl.BlockSpec((d, d), lambda b, i: (0, 0)),
            up_spec, up_spec, down_spec,
        ],
        out_specs=[pl.BlockSpec((1, tm, d), lambda b, i: (b, i, 0)), up_spec, up_spec, down_spec],
        scratch_shapes=[pltpu.VMEM((SUBLANES, d), F32)],
        compiler_params=_cparams(("arbitrary", "arbitrary")),
        name="conv_mixer",
    )(x, mods, g, w_in, conv_w, w_out, e_gate, e_up, e_down)


def _ffn_kernel(x_ref, mod_ref, g_ref, wg_ref, wu_ref, wd_ref, o_ref):
    b = pl.program_id(0)
    d = x_ref.shape[2]
    x = x_ref[0]
    shift, scale, gate = _split_mod(mod_ref, b, d)
    h = _norm_mod(x, g_ref[...], scale, shift).astype(BF16)
    a = _silu(jnp.dot(h, wg_ref[...], preferred_element_type=F32)) * jnp.dot(
        h, wu_ref[...], preferred_element_type=F32)
    y = jnp.dot(a.astype(BF16), wd_ref[...], preferred_element_type=F32)
    o_ref[0] = x + gate * y


def _dense_ffn(x, mods, set_idx, g, w_gate, w_up, w_down, tm):
    bsz, s, d = x.shape
    f = w_gate.shape[1]
    resident = pl.Buffered(1)
    return pl.pallas_call(
        _ffn_kernel,
        out_shape=jax.ShapeDtypeStruct(x.shape, F32),
        grid=(bsz, s // tm),
        in_specs=[
            pl.BlockSpec((1, tm, d), lambda b, i: (b, i, 0)),
            pl.BlockSpec((1, SUBLANES, 3 * d), lambda b, i: (set_idx, 0, 0)),
            pl.BlockSpec((1, d), lambda b, i: (0, 0)),
            pl.BlockSpec((d, f), lambda b, i: (0, 0), pipeline_mode=resident),
            pl.BlockSpec((d, f), lambda b, i: (0, 0), pipeline_mode=resident),
            pl.BlockSpec((f, d), lambda b, i: (0, 0), pipeline_mode=resident),
        ],
        out_specs=pl.BlockSpec((1, tm, d), lambda b, i: (b, i, 0)),
        compiler_params=_cparams(("parallel", "parallel")),
        name="dense_ffn",
    )(x, mods, g, w_gate, w_up, w_down)


def _class_major(ref, tm, r_from, r_to):
    n_chunks = ref.shape[0]
    if r_to == r_from:
        return jnp.concatenate([ref[c] for c in range(n_chunks)], axis=1)

    def rows(res):
        start = (res % r_from) * (tm // r_from) + res // r_from
        return pl.ds(start, tm // r_to, stride=r_to // r_from)

    return jnp.concatenate(
        [jnp.concatenate([ref[c, rows(res), :] for c in range(n_chunks)], axis=1)
         for res in range(r_to)], axis=0)


def _split_matmul(a, b01):
    hi = a.astype(BF16)
    lo = (a - hi.astype(F32)).astype(BF16)
    return jnp.dot(hi, b01, preferred_element_type=F32) + jnp.dot(lo, b01, preferred_element_type=F32)


def _rotary_tables(pos, freq, tm):
    half = HEAD_DIM // 2
    per_row = LANES // half
    q4 = tm // per_row
    lane = lax.broadcasted_iota(jnp.int32, (1, LANES), 1)
    pos4 = pos[(per_row - 1) * q4:]
    for j in reversed(range(per_row - 1)):
        pos4 = jnp.where(lane // half == j, pos[j * q4:(j + 1) * q4], pos4)
    ang = pos4 * freq
    cos4, sin4 = jnp.cos(ang), jnp.sin(ang)
    li = lax.broadcasted_iota(jnp.int32, (LANES, LANES), 0)
    lj = lax.broadcasted_iota(jnp.int32, (LANES, LANES), 1)
    same_freq = li % half == lj % half
    sign = jnp.where(lj % HEAD_DIM < half, -1.0, 1.0)
    cos_rows, sin_rows = [], []
    for j in range(per_row):
        pick = same_freq & (li // half == j)
        cos_rows.append(_split_matmul(cos4, jnp.where(pick, 1.0, 0.0).astype(BF16)))
        sin_rows.append(_split_matmul(sin4, jnp.where(pick, sign, 0.0).astype(BF16)))
    return jnp.concatenate(cos_rows, axis=0), jnp.concatenate(sin_rows, axis=0)


def _store_lane_chunks(ref, val):
    for c in range(ref.shape[0]):
        ref[c] = val[:, c * LANES:(c + 1) * LANES]


def _rope(t, cos, sin_signed, first_half):
    out = []
    for c in range(t.shape[1] // LANES):
        tc = t[:, c * LANES:(c + 1) * LANES]
        rot = jnp.where(first_half, pltpu.roll(tc, LANES - HEAD_DIM // 2, 1),
                        pltpu.roll(tc, HEAD_DIM // 2, 1))
        out.append(tc * cos + rot * sin_signed)
    return jnp.concatenate(out, axis=1)


def _qkv_kernel(x_ref, mod_ref, g_ref, pos_ref, freq_ref, w_ref, o0_ref, o1_ref, o2_ref,
                scr_a, scr_b):
    b = pl.program_id(0)
    tm, d = x_ref.shape[1], x_ref.shape[2]
    shift, scale, _ = _split_mod(mod_ref, b, d)
    first_half = (lax.broadcasted_iota(jnp.int32, (1, LANES), 1) % HEAD_DIM) < HEAD_DIM // 2
    cos_t, sin_t = _rotary_tables(pos_ref[0].astype(F32), freq_ref[...], tm)
    _store_lane_chunks(scr_a, jnp.concatenate(
        [_norm_mod(x_ref[0], g_ref[...], scale, shift), cos_t, sin_t], axis=1))
    scrs, cur, cur_r = (scr_a, scr_b), 0, 1
    gl = GROUP_LANES
    for gi, ((_, r), o_ref) in enumerate(zip(ATTN_GROUPS, (o0_ref, o1_ref, o2_ref))):
        vals = _class_major(scrs[cur], tm, cur_r, r)
        if r != cur_r and gi + 1 < len(ATTN_GROUPS):
            cur, cur_r = 1 - cur, r
            _store_lane_chunks(scrs[cur], vals)
        hg, cos, sin_signed = vals[:, :d].astype(BF16), vals[:, d:d + LANES], vals[:, d + LANES:]
        qkv = jnp.dot(hg, w_ref[gi], preferred_element_type=F32)
        q = _rope(qkv[:, :gl], cos, sin_signed, first_half) * (HEAD_DIM ** -0.5)
        k = _rope(qkv[:, gl:2 * gl], cos, sin_signed, first_half)
        o_ref[0, :, :gl] = q.astype(BF16)
        o_ref[0, :, gl:2 * gl] = k.astype(BF16)
        o_ref[0, :, 2 * gl:] = qkv[:, 2 * gl:].astype(BF16)


def _qkv_rope(x, mods, set_idx, g, pos3, freq, w_groups, tm):
    bsz, s, d = x.shape
    n_groups = len(ATTN_GROUPS)
    out = jax.ShapeDtypeStruct((bsz, s, 3 * GROUP_LANES), BF16)
    return pl.pallas_call(
        _qkv_kernel,
        out_shape=[out] * n_groups,
        grid=(bsz, s // tm),
        in_specs=[
            pl.BlockSpec((1, tm, d), lambda b, i: (b, i, 0)),
            pl.BlockSpec((1, SUBLANES, 3 * d), lambda b, i: (set_idx, 0, 0)),
            pl.BlockSpec((1, d), lambda b, i: (0, 0)),
            pl.BlockSpec((1, tm, 1), lambda b, i: (b, i, 0)),
            pl.BlockSpec((1, LANES), lambda b, i: (0, 0)),
            pl.BlockSpec((n_groups, d, 3 * GROUP_LANES), lambda b, i: (0, 0, 0)),
        ],
        out_specs=[pl.BlockSpec((1, tm, 3 * GROUP_LANES), lambda b, i: (b, i, 0))] * n_groups,
        scratch_shapes=[pltpu.VMEM((d // LANES + 2, tm, LANES), F32)] * 2,
        compiler_params=_cparams(("parallel", "parallel")),
        name="qkv_rope",
    )(x, mods, g, pos3, freq, w_groups)


def _store_rows(ref, a, val):
    st, per_tile = ATTN_STEPS, ref.shape[1]
    if per_tile >= st:
        ref[(a * st) // per_tile, pl.ds((a * st) % per_tile, st), :] = val
    else:
        for j in range(st // per_tile):
            ref[a * (st // per_tile) + j] = val[j * per_tile:(j + 1) * per_tile]


def _attn_kernel(q_ref, kc_ref, vc_ref, kp_ref, vp_ref, o_ref, lse_ref):
    i = pl.program_id(2)
    st = ATTN_STEPS
    tq = q_ref.shape[0] * q_ref.shape[1]
    q = q_ref[...].reshape(tq, GROUP_LANES)
    k_rows = jnp.concatenate([kp_ref[...].reshape(st, GROUP_LANES),
                              kc_ref[...].reshape(tq, GROUP_LANES)], axis=0)
    v_rows = jnp.concatenate([vp_ref[...].reshape(st, GROUP_LANES),
                              vc_ref[...].reshape(tq, GROUP_LANES)], axis=0)
    lane = lax.broadcasted_iota(jnp.int32, (1, LANES), 1)
    qi = lax.broadcasted_iota(jnp.int32, (st, 2 * st), 0)
    kj = lax.broadcasted_iota(jnp.int32, (st, 2 * st), 1)
    band = (kj >= qi) & (kj <= qi + st)
    band_first = band & ((kj >= st) | (i > 0))
    nt_dims = (((1,), (1,)), ((), ()))
    zero = jnp.zeros((), BF16)

    k_heads, v_heads = [], []
    for pair in range(PAIRS_PER_GROUP):
        sl = slice(pair * LANES, (pair + 1) * LANES)
        kp_, vp_ = k_rows[:, sl], v_rows[:, sl]
        if HEADS_PER_GROUP - 2 * pair >= 2:
            halves = [lane < HEAD_DIM, lane >= HEAD_DIM]
            k_heads.append([jnp.where(m, kp_, zero) for m in halves])
            v_heads.append([jnp.where(m, vp_, zero) for m in halves])
        else:
            k_heads.append([kp_])
            v_heads.append([vp_])

    blocks = [(a, pair, hh) for a in range(tq // st) for pair in range(PAIRS_PER_GROUP)
              for hh in range(len(k_heads[pair]))]
    scores = []
    for a, pair, hh in blocks:
        qa = q[a * st:(a + 1) * st, pair * LANES:(pair + 1) * LANES]
        s = lax.dot_general(qa, k_heads[pair][hh][a * st:(a + 2) * st], nt_dims,
                            preferred_element_type=F32)
        scores.append(jnp.where(band if a else band_first, s, NEG_INF))
    s_all = jnp.concatenate(scores, axis=0)
    m_all = jnp.max(s_all, axis=-1, keepdims=True)
    p_all = jnp.exp(s_all - m_all)
    l_all = jnp.sum(p_all, axis=-1, keepdims=True)
    p_all = p_all.astype(BF16)
    inv_all = 1.0 / l_all
    lse_all = m_all + jnp.log(l_all)

    for a in range(tq // st):
        lse_tile = jnp.zeros((st, LANES), F32)
        o_tiles = []
        for pair in range(PAIRS_PER_GROUP):
            acc = jnp.zeros((st, LANES), F32)
            for hh in range(len(k_heads[pair])):
                rows = slice(blocks.index((a, pair, hh)) * st, (blocks.index((a, pair, hh)) + 1) * st)
                pv = jnp.dot(p_all[rows], v_heads[pair][hh][a * st:(a + 2) * st],
                             preferred_element_type=F32)
                acc = acc + pv * inv_all[rows]
                lse_tile = jnp.where(lane == 2 * pair + hh, lse_all[rows], lse_tile)
            o_tiles.append(acc.astype(BF16))
        _store_rows(o_ref, a, jnp.concatenate(o_tiles, axis=1))
        _store_rows(lse_ref, a, lse_tile)


def _dilated_attn(qkv, r, tmp):
    bsz, s, width = qkv.shape
    gl, st = GROUP_LANES, ATTN_STEPS
    cls_rows = tmp // r
    tq = min(1024, s // r)
    nt = tq // cls_rows
    n_tiles = s // tmp
    n_q = (s // r) // tq
    sub = tq // st
    main = qkv.reshape(bsz, n_tiles, r, cls_rows, width)

    def main_spec(col):
        return pl.BlockSpec((None, nt, None, cls_rows, gl), lambda b, res, i: (b, i, res, 0, col))

    if cls_rows >= st:
        per_tile = cls_rows // st
        prev_arr = qkv.reshape(bsz, n_tiles, r, per_tile, st, width)

        def prev_spec(col):
            def index(b, res, i):
                c = jnp.maximum(i * sub - 1, 0)
                return (b, c // per_tile, res, c % per_tile, 0, col)
            return pl.BlockSpec((None, None, None, None, st, gl), index)
    else:
        ntp = st // cls_rows
        prev_arr = qkv.reshape(bsz, n_tiles // ntp, ntp, r, cls_rows, width)

        def prev_spec(col):
            return pl.BlockSpec((None, None, ntp, None, cls_rows, gl),
                                lambda b, res, i: (b, jnp.maximum(i * sub - 1, 0), 0, res, 0, col))

    o5 = jax.ShapeDtypeStruct((bsz, n_tiles, r, cls_rows, gl), BF16)
    l5 = jax.ShapeDtypeStruct((bsz, n_tiles, r, cls_rows, LANES), F32)

    o, lse = pl.pallas_call(
        _attn_kernel,
        out_shape=[o5, l5],
        grid=(bsz, r, n_q),
        in_specs=[main_spec(0), main_spec(1), main_spec(2), prev_spec(1), prev_spec(2)],
        out_specs=[
            pl.BlockSpec((None, nt, None, cls_rows, gl), lambda b, res, i: (b, i, res, 0, 0)),
            pl.BlockSpec((None, nt, None, cls_rows, LANES), lambda b, res, i: (b, i, res, 0, 0)),
        ],
        compiler_params=_cparams(("parallel", "parallel", "parallel")),
        name=f"dilated_attn_r{r}",
    )(main, main, main, prev_arr, prev_arr)
    return o.reshape(bsz, s, gl), lse.reshape(bsz, s, LANES)


def _token_major(scr, val, tm, r):
    if r == 1:
        return val
    n = tm // r
    for c in range(scr.shape[0]):
        for res in range(r):
            scr[c, pl.ds(res, n, stride=r), :] = val[res * n:(res + 1) * n, c * LANES:(c + 1) * LANES]
    return jnp.concatenate([scr[c] for c in range(scr.shape[0])], axis=1)


def _attn_out_route_kernel(x_ref, mod_a_ref, mod_m_ref, g_ref, o0_ref, o1_ref, o2_ref,
                           l0_ref, l1_ref, l2_ref, expand_ref, wo_ref, wr_ref,
                           x_out_ref, h_out_ref, route_ref, wts_ref, cnt_ref,
                           o_scr, l_scr, base_scr):
    b, i = pl.program_id(0), pl.program_id(1)
    tm, d = x_ref.shape[1], x_ref.shape[2]
    n_exp = cnt_ref.shape[0]

    @pl.when((b == 0) & (i == 0))
    def _():
        base_scr[...] = jnp.zeros_like(base_scr)

    o_refs, l_refs = (o0_ref, o1_ref, o2_ref), (l0_ref, l1_ref, l2_ref)
    outs, lses = [], []
    for gi, (_, r) in enumerate(ATTN_GROUPS):
        lses.append(_token_major(l_scr.at[gi], l_refs[gi][0], tm, r))
        outs.append(_token_major(o_scr.at[gi], o_refs[gi][0].astype(F32), tm, r))
    _, _, gate_a = _split_mod(mod_a_ref, b, d)
    shift, scale, _ = _split_mod(mod_m_ref, b, d)
    lane = lax.broadcasted_iota(jnp.int32, (1, LANES), 1)
    n_slab = d // LANES
    rc = tm // ROUTE_ROW_CHUNKS
    ri = lax.broadcasted_iota(jnp.int32, (rc, rc), 0)
    ci = lax.broadcasted_iota(jnp.int32, (rc, rc), 1)
    before = jnp.where(ci < ri, 1.0, 0.0).astype(BF16)
    count = base_scr[0:1, :]

    for c in range(ROUTE_ROW_CHUNKS):
        rows = slice(c * rc, (c + 1) * rc)
        ls = [l[rows] for l in lses]
        m = jnp.maximum(jnp.maximum(ls[0], ls[1]), ls[2])
        es = [jnp.exp(l - m) for l in ls]
        inv = 1.0 / (es[0] + es[1] + es[2])
        scaled = []
        for gi in range(len(ATTN_GROUPS)):
            alpha = es[gi] * inv
            hi = alpha.astype(BF16)
            lo = (alpha - hi.astype(F32)).astype(BF16)
            wide = jnp.dot(hi, expand_ref[...], preferred_element_type=F32) + jnp.dot(
                lo, expand_ref[...], preferred_element_type=F32)
            scaled.append((outs[gi][rows] * wide).astype(BF16))
        attn = jnp.dot(jnp.concatenate(scaled, axis=1), wo_ref[...], preferred_element_type=F32)
        x = x_ref[0, rows, :] + gate_a * attn
        x_out_ref[0, rows, :] = x

        h = _norm_mod(x, g_ref[...], scale, shift)
        for k in range(n_slab):
            h_out_ref[pl.ds(c * rc * n_slab + k, rc, stride=n_slab), :] = h[:, k * LANES:(k + 1) * LANES]

        h_hi = h.astype(BF16)
        h_lo = (h - h_hi.astype(F32)).astype(BF16)
        logits = (jnp.dot(h_hi, wr_ref[0], preferred_element_type=F32)
                  + jnp.dot(h_lo, wr_ref[0], preferred_element_type=F32)
                  + jnp.dot(h_hi, wr_ref[1], preferred_element_type=F32))
        logits = jnp.where(lane < n_exp, logits, NEG_INF)
        m1 = jnp.max(logits, axis=-1, keepdims=True)
        i1 = jnp.min(jnp.where(logits == m1, lane, LANES), axis=-1, keepdims=True)
        rest = jnp.where(lane == i1, NEG_INF, logits)
        m2 = jnp.max(rest, axis=-1, keepdims=True)
        i2 = jnp.min(jnp.where(rest == m2, lane, LANES), axis=-1, keepdims=True)
        e2 = jnp.exp(m2 - m1)
        w1 = 1.0 / (1.0 + e2)
        w2 = e2 * w1
        wts_ref[rows, :] = jnp.where(lane == 0, w1, jnp.where(lane == 1, w2, 0.0))

        oh1, oh2 = (lane == i1), (lane == i2)
        picks = jnp.where(oh1 | oh2, 1.0, 0.0).astype(BF16)
        rank = jnp.dot(before, picks, preferred_element_type=F32) + count
        r1 = jnp.sum(jnp.where(oh1, rank, 0.0), axis=-1, keepdims=True).astype(jnp.int32)
        r2 = jnp.sum(jnp.where(oh2, rank, 0.0), axis=-1, keepdims=True).astype(jnp.int32)
        route = jnp.where(lane == 0, r1, jnp.where(lane == 1, r2, jnp.where(
            lane == 2, i1, jnp.where(lane == 3, i2, 0))))
        route_ref[:, rows] = jnp.transpose(route)[:SUBLANES, :]
        count = count + jnp.sum(picks.astype(F32), axis=0, keepdims=True)

    base_scr[...] = jnp.broadcast_to(count, base_scr.shape)
    cnt_ref[...] = jnp.broadcast_to(count, cnt_ref.shape).astype(jnp.int32)


def _attn_out_route(x, mods, set_a, set_m, g, outs, lses, expand, w_o, w_router, tm):
    bsz, s, d = x.shape
    t = bsz * s
    n_i = s // tm
    n_groups = len(ATTN_GROUPS)
    gl = GROUP_LANES
    tok = lambda b, i: (b, i, 0)
    flat = lambda b, i: (b * n_i + i, 0)
    const2 = lambda b, i: (0, 0)
    return pl.pallas_call(
        _attn_out_route_kernel,
        out_shape=[
            jax.ShapeDtypeStruct((bsz, s, d), F32),
            jax.ShapeDtypeStruct((t * ROW_SLAB, LANES), F32),
            jax.ShapeDtypeStruct((SUBLANES, t), jnp.int32),
            jax.ShapeDtypeStruct((t, LANES), F32),
            jax.ShapeDtypeStruct((SUBLANES, LANES), jnp.int32),
        ],
        grid=(bsz, n_i),
        in_specs=[
            pl.BlockSpec((1, tm, d), tok),
            pl.BlockSpec((1, SUBLANES, 3 * d), lambda b, i: (set_a, 0, 0)),
            pl.BlockSpec((1, SUBLANES, 3 * d), lambda b, i: (set_m, 0, 0)),
            pl.BlockSpec((1, d), const2),
        ] + [pl.BlockSpec((1, tm, gl), tok)] * n_groups
          + [pl.BlockSpec((1, tm, LANES), tok)] * n_groups + [
            pl.BlockSpec((LANES, gl), const2),
            pl.BlockSpec((n_groups * gl, d), const2),
            pl.BlockSpec((2, d, LANES), lambda b, i: (0, 0, 0)),
        ],
        out_specs=[
            pl.BlockSpec((1, tm, d), tok),
            pl.BlockSpec((tm * ROW_SLAB, LANES), flat),
            pl.BlockSpec((SUBLANES, tm), lambda b, i: (0, b * n_i + i)),
            pl.BlockSpec((tm, LANES), flat),
            pl.BlockSpec((SUBLANES, LANES), const2),
        ],
        scratch_shapes=[pltpu.VMEM((n_groups, gl // LANES, tm, LANES), F32),
                        pltpu.VMEM((n_groups, 1, tm, LANES), F32),
                        pltpu.VMEM((SUBLANES, LANES), F32)],
        compiler_params=_cparams(("arbitrary", "arbitrary")),
        name="attn_out_route",
    )(x, mods, mods, g, *outs, *lses, expand, w_o, w_router)


DMA_LOOP_UNROLL = 16


def _for_each(n, fn):
    def body(g, carry):
        for u in range(DMA_LOOP_UNROLL):
            fn(g * DMA_LOOP_UNROLL + u)
        return carry
    lax.fori_loop(0, n // DMA_LOOP_UNROLL, body, 0)


def _row_copy(src, src_row, dst, dst_row, sem):
    return pltpu.make_async_copy(src.at[pl.ds(pl.multiple_of(src_row * ROW_SLAB, ROW_SLAB), ROW_SLAB)],
                                 dst.at[pl.ds(pl.multiple_of(dst_row * ROW_SLAB, ROW_SLAB), ROW_SLAB)],
                                 sem)


def _dispatch_kernel(fill_ref, pos_ref, h_ref, xs_hbm, zeros_scr, sem, zsem, *, tmd, te, n_exp,
                     n_tiles):
    step = pl.program_id(0)

    def zero_fills():
        def zero_copy(first_row, n):
            start = pl.multiple_of(first_row * ROW_SLAB, ROW_SLAB)
            return pltpu.make_async_copy(zeros_scr.at[pl.ds(0, n * ROW_SLAB)],
                                         xs_hbm.at[pl.ds(start, n * ROW_SLAB)], zsem)
        fills = []
        for e in range(n_exp):
            row, length = fill_ref[e], fill_ref[n_exp + e]
            for bit in reversed(range(te.bit_length() - 1)):
                take = (length & (1 << bit)) != 0
                fills.append((take, zero_copy(row, 1 << bit)))
                row = row + jnp.where(take, 1 << bit, 0)
        n_used = fill_ref[2 * n_exp]
        for k in range(n_exp):
            fills.append((n_used + k < n_tiles, zero_copy((n_used + k) * te, te)))
        return fills

    @pl.when(step == 0)
    def _():
        zeros_scr[...] = jnp.zeros_like(zeros_scr)
        for take, cp in zero_fills():
            pl.when(take)(cp.start)
        for take, cp in zero_fills():
            pl.when(take)(cp.wait)

    def copies(tok):
        return [_row_copy(h_ref, tok, xs_hbm, pos_ref[0, 0, k * tmd + tok], sem)
                for k in range(TOP_K)]

    _for_each(tmd, lambda tok: [cp.start(priority=k) for k, cp in enumerate(copies(tok))])
    _for_each(tmd, lambda tok: [cp.wait() for cp in copies(tok)])


def _moe_dispatch(fill, pos_blocks, h_rows, n_tiles, n_exp, tmd, te):
    n_steps = pos_blocks.shape[0]
    assert te & (te - 1) == 0
    kernel = functools.partial(_dispatch_kernel, tmd=tmd, te=te, n_exp=n_exp, n_tiles=n_tiles)
    return pl.pallas_call(
        kernel,
        out_shape=jax.ShapeDtypeStruct((n_tiles * te * ROW_SLAB, LANES), F32),
        grid_spec=pltpu.PrefetchScalarGridSpec(
            num_scalar_prefetch=1,
            grid=(n_steps,),
            in_specs=[
                pl.BlockSpec((1, 1, tmd * TOP_K), lambda s, cnt: (s, 0, 0), memory_space=pltpu.SMEM),
                pl.BlockSpec((tmd * ROW_SLAB, LANES), lambda s, cnt: (s, 0)),
            ],
            out_specs=pl.BlockSpec(memory_space=pl.ANY),
            scratch_shapes=[pltpu.VMEM((te * ROW_SLAB, LANES), F32), pltpu.SemaphoreType.DMA,
                            pltpu.SemaphoreType.DMA],
        ),
        compiler_params=_cparams(("arbitrary",)),
        name="moe_dispatch",
    )(fill, pos_blocks, h_rows)


def _slab_rows_to_matrix(ref, n_rows, n_chunks):
    return jnp.concatenate([ref[pl.ds(c, n_rows, stride=n_chunks), :] for c in range(n_chunks)],
                           axis=1)


def _experts_kernel(tile_expert_ref, n_used_ref, x_ref, wg_ref, wu_ref, wd_ref, y_ref, *, te,
                    f_chunks):
    del tile_expert_ref
    used = pl.program_id(0) < n_used_ref[0]

    @pl.when(jnp.logical_not(used))
    def _():
        y_ref[...] = jnp.zeros_like(y_ref)

    @pl.when(used)
    def _():
        d = wg_ref.shape[1]
        n_chunks = d // LANES
        x = _slab_rows_to_matrix(x_ref, te, n_chunks).astype(BF16)
        y = jnp.zeros((te, d), F32)
        for lo, hi in f_chunks:
            gate = jnp.dot(x, wg_ref[0, :, lo:hi], preferred_element_type=F32)
            up = jnp.dot(x, wu_ref[0, :, lo:hi], preferred_element_type=F32)
            y = y + jnp.dot((_silu(gate) * up).astype(BF16), wd_ref[0, lo:hi, :],
                            preferred_element_type=F32)
        for c in range(n_chunks):
            y_ref[pl.ds(c, te, stride=n_chunks), :] = y[:, c * LANES:(c + 1) * LANES]


def _f_chunks(f):
    if f <= 1536:
        return ((0, f),)
    half = (f // 2 + 255) // 256 * 256
    return ((0, half), (half, f))


def _moe_experts(tile_expert, n_used, xs, w_gate, w_up, w_down, te):
    n_tiles = tile_expert.shape[0]
    _, d, f = w_gate.shape
    kernel = functools.partial(_experts_kernel, te=te, f_chunks=_f_chunks(f))
    weights = lambda i, te_, nu_: (te_[i], 0, 0)
    return pl.pallas_call(
        kernel,
        out_shape=jax.ShapeDtypeStruct(xs.shape, F32),
        grid_spec=pltpu.PrefetchScalarGridSpec(
            num_scalar_prefetch=2,
            grid=(n_tiles,),
            in_specs=[
                pl.BlockSpec((te * ROW_SLAB, LANES), lambda i, te_, nu_: (jnp.minimum(i, nu_[0] - 1), 0)),
                pl.BlockSpec((1, d, f), weights),
                pl.BlockSpec((1, d, f), weights),
                pl.BlockSpec((1, f, d), weights),
            ],
            out_specs=pl.BlockSpec((te * ROW_SLAB, LANES), lambda i, te_, nu_: (i, 0)),
        ),
        compiler_params=_cparams(("arbitrary",)),
        name="moe_experts",
    )(tile_expert, n_used, xs, w_gate, w_up, w_down)


def _combine_kernel(pos_ref, pos_next_ref, x_ref, mod_ref, g_ref, wts_ref, ys_hbm, o_ref, buf, sems,
                    *, tmc):
    b = pl.program_id(0)
    step = b * pl.num_programs(1) + pl.program_id(1)
    n_steps = pl.num_programs(0) * pl.num_programs(1)
    slot = step % 2
    d = x_ref.shape[2]
    n_chunks = d // LANES

    def copies(p_ref, slot_, tok):
        return [_row_copy(ys_hbm, p_ref[0, 0, k * tmc + tok], buf.at[slot_], k * tmc + tok,
                          sems.at[slot_]) for k in range(TOP_K)]

    def fetch(p_ref, slot_):
        _for_each(tmc, lambda tok: [cp.start(priority=k)
                                    for k, cp in enumerate(copies(p_ref, slot_, tok))])

    pl.when(step == 0)(lambda: fetch(pos_ref, slot))
    pl.when(step + 1 < n_steps)(lambda: fetch(pos_next_ref, 1 - slot))
    _for_each(tmc, lambda tok: [cp.wait() for cp in copies(pos_ref, slot, tok)])

    wts = wts_ref[...]
    y = jnp.zeros((tmc, d), F32)
    rows = buf.at[slot]
    for k in range(TOP_K):
        yk = _slab_rows_to_matrix(rows.at[pl.ds(k * tmc * ROW_SLAB, tmc * ROW_SLAB)], tmc, n_chunks)
        y = y + wts[:, k:k + 1] * yk
    _, _, gate = _split_mod(mod_ref, b, d)
    x = x_ref[0] + gate * y
    o_ref[0] = x * lax.rsqrt(jnp.mean(x * x, axis=-1, keepdims=True) + RMS_EPS) * g_ref[...]


def _moe_combine(pos_blocks, x, mods, set_idx, final_g, wts, ys, tmc):
    bsz, s, d = x.shape
    n_i = s // tmc
    last = bsz * n_i - 1
    kernel = functools.partial(_combine_kernel, tmc=tmc)
    return pl.pallas_call(
        kernel,
        out_shape=jax.ShapeDtypeStruct(x.shape, F32),
        grid=(bsz, n_i),
        in_specs=[
            pl.BlockSpec((1, 1, tmc * TOP_K), lambda b, i: (b * n_i + i, 0, 0),
                         memory_space=pltpu.SMEM),
            pl.BlockSpec((1, 1, tmc * TOP_K), lambda b, i: (jnp.minimum(b * n_i + i + 1, last), 0, 0),
                         memory_space=pltpu.SMEM),
            pl.BlockSpec((1, tmc, d), lambda b, i: (b, i, 0)),
            pl.BlockSpec((1, SUBLANES, 3 * d), lambda b, i: (set_idx, 0, 0)),
            pl.BlockSpec((1, d), lambda b, i: (0, 0)),
            pl.BlockSpec((tmc, LANES), lambda b, i: (b * n_i + i, 0)),
            pl.BlockSpec(memory_space=pl.ANY),
        ],
        out_specs=pl.BlockSpec((1, tmc, d), lambda b, i: (b, i, 0)),
        scratch_shapes=[pltpu.VMEM((2, TOP_K * tmc * ROW_SLAB, LANES), F32),
                        pltpu.SemaphoreType.DMA((2,))],
        compiler_params=_cparams(("arbitrary", "arbitrary")),
        name="moe_combine",
    )(pos_blocks, pos_blocks, x, mods, final_g, wts, ys)


def _pad_group_columns(w, which):
    width = HEADS_PER_GROUP * HEAD_DIM
    n_groups = len(ATTN_GROUPS)
    cols = w[:, which * n_groups * width:(which + 1) * n_groups * width]
    cols = cols.reshape(w.shape[0], n_groups, width)
    return jnp.pad(cols, ((0, 0), (0, 0), (0, GROUP_LANES - width)))


def kernel(x, c, positions, mod_w, mod_b, norm_g, conv_w_in, conv_w, conv_w_out, ffn_w_gate,
           ffn_w_up, ffn_w_down, attn_w_qkv, attn_w_o, router_w, moe_w_gate, moe_w_up, moe_w_down,
           final_g):
    bsz, s, d = x.shape
    t = bsz * s
    n_groups = len(ATTN_GROUPS)
    n_exp = router_w.shape[-1]
    assert all(w // r == ATTN_STEPS for w, r in ATTN_GROUPS)
    assert bsz <= SUBLANES and n_exp <= SUBLANES and d == ROW_SLAB * LANES
    tm = min(512, s)
    assert s % tm == 0 and all(s % (r * ATTN_STEPS) == 0 and tm % r == 0 for _, r in ATTN_GROUPS)

    c_pad = jnp.pad(c.astype(F32), ((0, SUBLANES - bsz), (0, 0)))
    mods = _ada_params(c_pad, mod_w.reshape(-1, d, 3 * d), mod_b.reshape(-1, 1, 3 * d))

    x, e_gate, e_up, e_down = _conv_mixer(
        x, mods, 0, norm_g[0, 0][None], conv_w_in[0].astype(BF16), conv_w[0],
        conv_w_out[0].astype(BF16), moe_w_gate[0], moe_w_up[0], moe_w_down[0], tm)
    x = _dense_ffn(x, mods, 1, norm_g[0, 1][None], ffn_w_gate[0].astype(BF16),
                   ffn_w_up[0].astype(BF16), ffn_w_down[0].astype(BF16), tm)

    w_qkv = attn_w_qkv[0]
    w_groups = jnp.concatenate([_pad_group_columns(w_qkv, which) for which in range(3)], axis=2)
    w_groups = w_groups.transpose(1, 0, 2).astype(BF16)
    inv_freq = ROPE_THETA ** (-jnp.arange(0, HEAD_DIM, 2, dtype=F32) / HEAD_DIM)
    freq = jnp.tile(inv_freq, LANES // (HEAD_DIM // 2))[None]
    qkvs = _qkv_rope(x, mods, 2, norm_g[1, 0][None], positions[..., None], freq, w_groups, tm)
    outs, lses = zip(*[_dilated_attn(qkv, r, tm) for qkv, (_, r) in zip(qkvs, ATTN_GROUPS)])

    width = HEADS_PER_GROUP * HEAD_DIM
    w_o = jnp.pad(attn_w_o[0].reshape(n_groups, width, d), ((0, 0), (0, GROUP_LANES - width), (0, 0)))
    w_o = w_o.reshape(n_groups * GROUP_LANES, d).astype(BF16)
    expand = (jnp.arange(GROUP_LANES)[None, :] // HEAD_DIM == jnp.arange(LANES)[:, None]).astype(BF16)
    w_router = jnp.pad(router_w[0], ((0, 0), (0, LANES - n_exp)))
    w_router_hi = w_router.astype(BF16)
    w_router = jnp.stack([w_router_hi, (w_router - w_router_hi.astype(F32)).astype(BF16)])

    x, h_rows, route, wts, counts = _attn_out_route(
        x, mods, 2, 3, norm_g[1, 1][None], outs, lses, expand, w_o, w_router, tm)

    te = min(512, t)
    n_tiles = (t * TOP_K) // te + n_exp
    counts = counts[0, :n_exp]
    tiles_per_expert = (counts + te - 1) // te
    ends = jnp.cumsum(tiles_per_expert)
    first_row = (ends - tiles_per_expert) * te
    n_used = ends[-1:]
    experts = route[TOP_K:2 * TOP_K]
    pos = route[:TOP_K] + sum(jnp.where(experts == e, first_row[e], 0) for e in range(n_exp))
    fill = jnp.concatenate([first_row + counts, tiles_per_expert * te - counts, n_used]).astype(jnp.int32)
    idx = jnp.minimum(jnp.arange(n_tiles, dtype=jnp.int32), n_used - 1)
    tile_expert = jnp.sum(idx[:, None] >= ends[None, :], axis=1).astype(jnp.int32)

    pos_blocks = pos.reshape(TOP_K, t // tm, tm).transpose(1, 0, 2).reshape(t // tm, 1, TOP_K * tm)
    xs = _moe_dispatch(fill, pos_blocks, h_rows, n_tiles, n_exp, tm, te)
    ys = _moe_experts(tile_expert, n_used.astype(jnp.int32), xs, e_gate, e_up, e_down, te)
    return _moe_combine(pos_blocks, x, mods, 3, final_g[None], wts, ys, tm)
```

```python
import functools
import math

import jax
import jax.numpy as jnp
from jax import lax
from jax.experimental import pallas as pl
from jax.experimental.pallas import tpu as pltpu

CONV_WIDTH = 3
ATTN_GROUPS = ((128, 1), (512, 4), (2048, 16))
HEADS_PER_GROUP = 5
HEAD_DIM = 64
ROPE_THETA = 10000.0
TOP_K = 2
RMS_EPS = 1e-6

LANES = 128
SUBLANES = 8
VMEM_LIMIT_BYTES = 56 * 1024 * 1024

GROUP_LANES = 384
PAIRS_PER_GROUP = GROUP_LANES // LANES
ATTN_STEPS = 128
ROW_SLAB = 8
ROUTE_ROW_CHUNKS = 2

F32 = jnp.float32
BF16 = jnp.bfloat16
NEG_INF = float("-inf")


def _cparams(semantics):
    return pltpu.CompilerParams(dimension_semantics=semantics, vmem_limit_bytes=VMEM_LIMIT_BYTES)


def _norm_mod(x, g, scale, shift):
    y = x * lax.rsqrt(jnp.mean(x * x, axis=-1, keepdims=True) + RMS_EPS)
    return (y * g) * (1.0 + scale) + shift


def _split_mod(mod_ref, b, d):
    row = mod_ref[0, pl.ds(b, 1), :]
    return row[:, :d], row[:, d:2 * d], row[:, 2 * d:]


def _silu(x):
    return x * jax.nn.sigmoid(x)


def _ada_kernel(c_ref, w_ref, b_ref, o_ref):
    s, w = _silu(c_ref[...]), w_ref[0]
    s_hi, w_hi = s.astype(BF16), w.astype(BF16)
    s_lo = (s - s_hi.astype(F32)).astype(BF16)
    w_lo = (w - w_hi.astype(F32)).astype(BF16)
    both = jnp.dot(jnp.concatenate([s_hi, s_lo], axis=0), w_hi, preferred_element_type=F32)
    o_ref[0] = (both[:SUBLANES] + both[SUBLANES:]
                + jnp.dot(s_hi, w_lo, preferred_element_type=F32) + b_ref[0])


def _ada_params(c_pad, mod_w, mod_b):
    n_sets, d, d3 = mod_w.shape
    tn = d
    return pl.pallas_call(
        _ada_kernel,
        out_shape=jax.ShapeDtypeStruct((n_sets, SUBLANES, d3), F32),
        grid=(n_sets, d3 // tn),
        in_specs=[
            pl.BlockSpec((SUBLANES, d), lambda s, j: (0, 0)),
            pl.BlockSpec((1, d, tn), lambda s, j: (s, 0, j)),
            pl.BlockSpec((1, 1, tn), lambda s, j: (s, 0, j)),
        ],
        out_specs=pl.BlockSpec((1, SUBLANES, tn), lambda s, j: (s, 0, j)),
        compiler_params=_cparams(("parallel", "parallel")),
        name="ada_params",
    )(c_pad, mod_w, mod_b)


def _conv_mixer_kernel(x_ref, mod_ref, g_ref, win_ref, cw_ref, wout_ref, eg_ref, eu_ref, ed_ref,
                       o_ref, eg_out, eu_out, ed_out, carry_ref):
    b, i = pl.program_id(0), pl.program_id(1)
    tm, d = x_ref.shape[1], x_ref.shape[2]
    for src, dst in ((eg_ref, eg_out), (eu_ref, eu_out), (ed_ref, ed_out)):
        dst[...] = src[...].astype(BF16)

    @pl.when(i == 0)
    def _():
        carry_ref[...] = jnp.zeros_like(carry_ref)

    x = x_ref[0]
    shift, scale, gate = _split_mod(mod_ref, b, d)
    h = _norm_mod(x, g_ref[...], scale, shift).astype(BF16)
    bcu = jnp.dot(h, win_ref[...], preferred_element_type=F32)
    b_gate, v = bcu[:, :d], bcu[:, d:2 * d] * bcu[:, 2 * d:]
    row = lax.broadcasted_iota(jnp.int32, (tm, 1), 0)
    prev = carry_ref[...]
    v1 = jnp.where(row == 0, prev[7:8], pltpu.roll(v, 1, 0))
    v2 = jnp.where(row == 0, prev[6:7], jnp.where(row == 1, prev[7:8], pltpu.roll(v, 2, 0)))
    cw = cw_ref[...]
    conv = cw[0:1] * v2 + cw[1:2] * v1 + cw[2:3] * v
    carry_ref[...] = v[tm - SUBLANES:, :]
    y = jnp.dot((b_gate * conv).astype(BF16), wout_ref[...], preferred_element_type=F32)
    o_ref[0] = x + gate * y


def _conv_mixer(x, mods, set_idx, g, w_in, conv_w, w_out, e_gate, e_up, e_down, tm):
    bsz, s, d = x.shape
    n_i = s // tm
    n_exp, _, f = e_gate.shape
    per_expert = (bsz * n_i) // n_exp
    sublane_pack = 2 * SUBLANES
    assert bsz * n_i == per_expert * n_exp
    assert d % (per_expert * sublane_pack) == 0 and f % (per_expert * sublane_pack) == 0
    w_slice = lambda b, i: ((b * n_i + i) // per_expert, (b * n_i + i) % per_expert, 0)
    up_spec = pl.BlockSpec((1, d // per_expert, f), w_slice)
    down_spec = pl.BlockSpec((1, f // per_expert, d), w_slice)
    return pl.pallas_call(
        _conv_mixer_kernel,
        out_shape=[jax.ShapeDtypeStruct(x.shape, F32), jax.ShapeDtypeStruct(e_gate.shape, BF16),
                   jax.ShapeDtypeStruct(e_up.shape, BF16), jax.ShapeDtypeStruct(e_down.shape, BF16)],
        grid=(bsz, n_i),
        in_specs=[
            pl.BlockSpec((1, tm, d), lambda b, i: (b, i, 0)),
            pl.BlockSpec((1, SUBLANES, 3 * d), lambda b, i: (set_idx, 0, 0)),
            pl.BlockSpec((1, d), lambda b, i: (0, 0)),
            pl.BlockSpec((d, 3 * d), lambda b, i: (0, 0)),
            pl.BlockSpec((CONV_WIDTH, d), lambda b, i: (0, 0)),
            pl.BlockSpec((d, d), lambda b, i: (0, 0)),
            up_spec, up_spec, down_spec,
        ],
        out_specs=[pl.BlockSpec((1, tm, d), lambda b, i: (b, i, 0)), up_spec, up_spec, down_spec],
        scratch_shapes=[pltpu.VMEM((SUBLANES, d), F32)],
        compiler_params=_cparams(("arbitrary", "arbitrary")),
        name="conv_mixer",
    )(x, mods, g, w_in, conv_w, w_out, e_gate, e_up, e_down)


def _ffn_kernel(x_ref, mod_ref, g_ref, wg_ref, wu_ref, wd_ref, o_ref):
    b = pl.program_id(0)
    d = x_ref.shape[2]
    x = x_ref[0]
    shift, scale, gate = _split_mod(mod_ref, b, d)
    h = _norm_mod(x, g_ref[...], scale, shift).astype(BF16)
    a = _silu(jnp.dot(h, wg_ref[...], preferred_element_type=F32)) * jnp.dot(
        h, wu_ref[...], preferred_element_type=F32)
    y = jnp.dot(a.astype(BF16), wd_ref[...], preferred_element_type=F32)
    o_ref[0] = x + gate * y


def _dense_ffn(x, mods, set_idx, g, w_gate, w_up, w_down, tm):
    bsz, s, d = x.shape
    f = w_gate.shape[1]
    resident = pl.Buffered(1)
    return pl.pallas_call(
        _ffn_kernel,
        out_shape=jax.ShapeDtypeStruct(x.shape, F32),
        grid=(bsz, s // tm),
        in_specs=[
            pl.BlockSpec((1, tm, d), lambda b, i: (b, i, 0)),
            pl.BlockSpec((1, SUBLANES, 3 * d), lambda b, i: (set_idx, 0, 0)),
            pl.BlockSpec((1, d), lambda b, i: (0, 0)),
            pl.BlockSpec((d, f), lambda b, i: (0, 0), pipeline_mode=resident),
            pl.BlockSpec((d, f), lambda b, i: (0, 0), pipeline_mode=resident),
            pl.BlockSpec((f, d), lambda b, i: (0, 0), pipeline_mode=resident),
        ],
        out_specs=pl.BlockSpec((1, tm, d), lambda b, i: (b, i, 0)),
        compiler_params=_cparams(("parallel", "parallel")),
        name="dense_ffn",
    )(x, mods, g, w_gate, w_up, w_down)


def _class_major(ref, tm, r_from, r_to):
    n_chunks = ref.shape[0]
    if r_to == r_from:
        return jnp.concatenate([ref[c] for c in range(n_chunks)], axis=1)

    def rows(res):
        start = (res % r_from) * (tm // r_from) + res // r_from
        return pl.ds(start, tm // r_to, stride=r_to // r_from)

    return jnp.concatenate(
        [jnp.concatenate([ref[c, rows(res), :] for c in range(n_chunks)], axis=1)
         for res in range(r_to)], axis=0)


def _split_matmul(a, b01):
    hi = a.astype(BF16)
    lo = (a - hi.astype(F32)).astype(BF16)
    return jnp.dot(hi, b01, preferred_element_type=F32) + jnp.dot(lo, b01, preferred_element_type=F32)


def _rotary_tables(pos, freq, tm):
    half = HEAD_DIM // 2
    per_row = LANES // half
    q4 = tm // per_row
    lane = lax.broadcasted_iota(jnp.int32, (1, LANES), 1)
    pos4 = pos[(per_row - 1) * q4:]
    for j in reversed(range(per_row - 1)):
        pos4 = jnp.where(lane // half == j, pos[j * q4:(j + 1) * q4], pos4)
    ang = pos4 * freq
    cos4, sin4 = jnp.cos(ang), jnp.sin(ang)
    li = lax.broadcasted_iota(jnp.int32, (LANES, LANES), 0)
    lj = lax.broadcasted_iota(jnp.int32, (LANES, LANES), 1)
    same_freq = li % half == lj % half
    sign = jnp.where(lj % HEAD_DIM < half, -1.0, 1.0)
    cos_rows, sin_rows = [], []
    for j in range(per_row):
        pick = same_freq & (li // half == j)
        cos_rows.append(_split_matmul(cos4, jnp.where(pick, 1.0, 0.0).astype(BF16)))
        sin_rows.append(_split_matmul(sin4, jnp.where(pick, sign, 0.0).astype(BF16)))
    return jnp.concatenate(cos_rows, axis=0), jnp.concatenate(sin_rows, axis=0)


def _store_lane_chunks(ref, val):
    for c in range(ref.shape[0]):
        ref[c] = val[:, c * LANES:(c + 1) * LANES]


def _rope(t, cos, sin_signed, first_half):
    out = []
    for c in range(t.shape[1] // LANES):
        tc = t[:, c * LANES:(c + 1) * LANES]
        rot = jnp.where(first_half, pltpu.roll(tc, LANES - HEAD_DIM // 2, 1),
                        pltpu.roll(tc, HEAD_DIM // 2, 1))
        out.append(tc * cos + rot * sin_signed)
    return jnp.concatenate(out, axis=1)


def _qkv_kernel(x_ref, mod_ref, g_ref, pos_ref, freq_ref, w_ref, o0_ref, o1_ref, o2_ref,
                scr_a, scr_b):
    b = pl.program_id(0)
    tm, d = x_ref.shape[1], x_ref.shape[2]
    shift, scale, _ = _split_mod(mod_ref, b, d)
    first_half = (lax.broadcasted_iota(jnp.int32, (1, LANES), 1) % HEAD_DIM) < HEAD_DIM // 2
    cos_t, sin_t = _rotary_tables(pos_ref[0].astype(F32), freq_ref[...], tm)
    _store_lane_chunks(scr_a, jnp.concatenate(
        [_norm_mod(x_ref[0], g_ref[...], scale, shift), cos_t, sin_t], axis=1))
    scrs, cur, cur_r = (scr_a, scr_b), 0, 1
    gl = GROUP_LANES
    for gi, ((_, r), o_ref) in enumerate(zip(ATTN_GROUPS, (o0_ref, o1_ref, o2_ref))):
        vals = _class_major(scrs[cur], tm, cur_r, r)
        if r != cur_r and gi + 1 < len(ATTN_GROUPS):
            cur, cur_r = 1 - cur, r
            _store_lane_chunks(scrs[cur], vals)
        hg, cos, sin_signed = vals[:, :d].astype(BF16), vals[:, d:d + LANES], vals[:, d + LANES:]
        qkv = jnp.dot(hg, w_ref[gi], preferred_element_type=F32)
        q = _rope(qkv[:, :gl], cos, sin_signed, first_half) * (HEAD_DIM ** -0.5)
        k = _rope(qkv[:, gl:2 * gl], cos, sin_signed, first_half)
        o_ref[0, :, :gl] = q.astype(BF16)
        o_ref[0, :, gl:2 * gl] = k.astype(BF16)
        o_ref[0, :, 2 * gl:] = qkv[:, 2 * gl:].astype(BF16)


def _qkv_rope(x, mods, set_idx, g, pos3, freq, w_groups, tm):
    bsz, s, d = x.shape
    n_groups = len(ATTN_GROUPS)
    out = jax.ShapeDtypeStruct((bsz, s, 3 * GROUP_LANES), BF16)
    return pl.pallas_call(
        _qkv_kernel,
        out_shape=[out] * n_groups,
        grid=(bsz, s // tm),
        in_specs=[
            pl.BlockSpec((1, tm, d), lambda b, i: (b, i, 0)),
            pl.BlockSpec((1, SUBLANES, 3 * d), lambda b, i: (set_idx, 0, 0)),
            pl.BlockSpec((1, d), lambda b, i: (0, 0)),
            pl.BlockSpec((1, tm, 1), lambda b, i: (b, i, 0)),
            pl.BlockSpec((1, LANES), lambda b, i: (0, 0)),
            pl.BlockSpec((n_groups, d, 3 * GROUP_LANES), lambda b, i: (0, 0, 0)),
        ],
        out_specs=[pl.BlockSpec((1, tm, 3 * GROUP_LANES), lambda b, i: (b, i, 0))] * n_groups,
        scratch_shapes=[pltpu.VMEM((d // LANES + 2, tm, LANES), F32)] * 2,
        compiler_params=_cparams(("parallel", "parallel")),
        name="qkv_rope",
    )(x, mods, g, pos3, freq, w_groups)


def _store_rows(ref, a, val):
    st, per_tile = ATTN_STEPS, ref.shape[1]
    if per_tile >= st:
        ref[(a * st) // per_tile, pl.ds((a * st) % per_tile, st), :] = val
    else:
        for j in range(st // per_tile):
            ref[a * (st // per_tile) + j] = val[j * per_tile:(j + 1) * per_tile]


def _attn_kernel(q_ref, kc_ref, vc_ref, kp_ref, vp_ref, o_ref, lse_ref):
    i = pl.program_id(2)
    st = ATTN_STEPS
    tq = q_ref.shape[0] * q_ref.shape[1]
    q = q_ref[...].reshape(tq, GROUP_LANES)
    k_rows = jnp.concatenate([kp_ref[...].reshape(st, GROUP_LANES),
                              kc_ref[...].reshape(tq, GROUP_LANES)], axis=0)
    v_rows = jnp.concatenate([vp_ref[...].reshape(st, GROUP_LANES),
                              vc_ref[...].reshape(tq, GROUP_LANES)], axis=0)
    lane = lax.broadcasted_iota(jnp.int32, (1, LANES), 1)
    qi = lax.broadcasted_iota(jnp.int32, (st, 2 * st), 0)
    kj = lax.broadcasted_iota(jnp.int32, (st, 2 * st), 1)
    band = (kj >= qi) & (kj <= qi + st)
    band_first = band & ((kj >= st) | (i > 0))
    nt_dims = (((1,), (1,)), ((), ()))
    zero = jnp.zeros((), BF16)

    k_heads, v_heads = [], []
    for pair in range(PAIRS_PER_GROUP):
        sl = slice(pair * LANES, (pair + 1) * LANES)
        kp_, vp_ = k_rows[:, sl], v_rows[:, sl]
        if HEADS_PER_GROUP - 2 * pair >= 2:
            halves = [lane < HEAD_DIM, lane >= HEAD_DIM]
            k_heads.append([jnp.where(m, kp_, zero) for m in halves])
            v_heads.append([jnp.where(m, vp_, zero) for m in halves])
        else:
            k_heads.append([kp_])
            v_heads.append([vp_])

    blocks = [(a, pair, hh) for a in range(tq // st) for pair in range(PAIRS_PER_GROUP)
              for hh in range(len(k_heads[pair]))]
    scores = []
    for a, pair, hh in blocks:
        qa = q[a * st:(a + 1) * st, pair * LANES:(pair + 1) * LANES]
        s = lax.dot_general(qa, k_heads[pair][hh][a * st:(a + 2) * st], nt_dims,
                            preferred_element_type=F32)
        scores.append(jnp.where(band if a else band_first, s, NEG_INF))
    s_all = jnp.concatenate(scores, axis=0)
    m_all = jnp.max(s_all, axis=-1, keepdims=True)
    p_all = jnp.exp(s_all - m_all)
    l_all = jnp.sum(p_all, axis=-1, keepdims=True)
    p_all = p_all.astype(BF16)
    inv_all = 1.0 / l_all
    lse_all = m_all + jnp.log(l_all)

    for a in range(tq // st):
        lse_tile = jnp.zeros((st, LANES), F32)
        o_tiles = []
        for pair in range(PAIRS_PER_GROUP):
            acc = jnp.zeros((st, LANES), F32)
            for hh in range(len(k_heads[pair])):
                rows = slice(blocks.index((a, pair, hh)) * st, (blocks.index((a, pair, hh)) + 1) * st)
                pv = jnp.dot(p_all[rows], v_heads[pair][hh][a * st:(a + 2) * st],
                             preferred_element_type=F32)
                acc = acc + pv * inv_all[rows]
                lse_tile = jnp.where(lane == 2 * pair + hh, lse_all[rows], lse_tile)
            o_tiles.append(acc.astype(BF16))
        _store_rows(o_ref, a, jnp.concatenate(o_tiles, axis=1))
        _store_rows(lse_ref, a, lse_tile)


def _dilated_attn(qkv, r, tmp):
    bsz, s, width = qkv.shape
    gl, st = GROUP_LANES, ATTN_STEPS
    cls_rows = tmp // r
    tq = min(1024, s // r)
    nt = tq // cls_rows
    n_tiles = s // tmp
    n_q = (s // r) // tq
    sub = tq // st
    main = qkv.reshape(bsz, n_tiles, r, cls_rows, width)

    def main_spec(col):
        return pl.BlockSpec((None, nt, None, cls_rows, gl), lambda b, res, i: (b, i, res, 0, col))

    if cls_rows >= st:
        per_tile = cls_rows // st
        prev_arr = qkv.reshape(bsz, n_tiles, r, per_tile, st, width)

        def prev_spec(col):
            def index(b, res, i):
                c = jnp.maximum(i * sub - 1, 0)
                return (b, c // per_tile, res, c % per_tile, 0, col)
            return pl.BlockSpec((None, None, None, None, st, gl), index)
    else:
        ntp = st // cls_rows
        prev_arr = qkv.reshape(bsz, n_tiles // ntp, ntp, r, cls_rows, width)

        def prev_spec(col):
            return pl.BlockSpec((None, None, ntp, None, cls_rows, gl),
                                lambda b, res, i: (b, jnp.maximum(i * sub - 1, 0), 0, res, 0, col))

    o5 = jax.ShapeDtypeStruct((bsz, n_tiles, r, cls_rows, gl), BF16)
    l5 = jax.ShapeDtypeStruct((bsz, n_tiles, r, cls_rows, LANES), F32)

    o, lse = pl.pallas_call(
        _attn_kernel,
        out_shape=[o5, l5],
        grid=(bsz, r, n_q),
        in_specs=[main_spec(0), main_spec(1), main_spec(2), prev_spec(1), prev_spec(2)],
        out_specs=[
            pl.BlockSpec((None, nt, None, cls_rows, gl), lambda b, res, i: (b, i, res, 0, 0)),
            pl.BlockSpec((None, nt, None, cls_rows, LANES), lambda b, res, i: (b, i, res, 0, 0)),
        ],
        compiler_params=_cparams(("parallel", "parallel", "parallel")),
        name=f"dilated_attn_r{r}",
    )(main, main, main, prev_arr, prev_arr)
    return o.reshape(bsz, s, gl), lse.reshape(bsz, s, LANES)


MAX_STORE_STRIDE = 4


def _token_major(scr, val, tm, r, mid=None):
    if r == 1:
        return val
    if r > MAX_STORE_STRIDE and mid is not None:
        r_mid, n = MAX_STORE_STRIDE, tm // r
        for c in range(mid.shape[0]):
            for res in range(r):
                start = (res % r_mid) * (tm // r_mid) + res // r_mid
                mid[c, pl.ds(start, n, stride=r // r_mid), :] = (
                    val[res * n:(res + 1) * n, c * LANES:(c + 1) * LANES])
        val, r = jnp.concatenate([mid[c] for c in range(mid.shape[0])], axis=1), r_mid
    n = tm // r
    for c in range(scr.shape[0]):
        for res in range(r):
            scr[c, pl.ds(res, n, stride=r), :] = val[res * n:(res + 1) * n, c * LANES:(c + 1) * LANES]
    return jnp.concatenate([scr[c] for c in range(scr.shape[0])], axis=1)


def _attn_out_route_kernel(x_ref, mod_a_ref, mod_m_ref, g_ref, o0_ref, o1_ref, o2_ref,
                           l0_ref, l1_ref, l2_ref, expand_ref, wo_ref, wr_ref,
                           x_out_ref, h_out_ref, route_ref, wts_ref, cnt_ref,
                           o_scr, l_scr, base_scr):
    b, i = pl.program_id(0), pl.program_id(1)
    tm, d = x_ref.shape[1], x_ref.shape[2]
    n_exp = cnt_ref.shape[0]

    @pl.when((b == 0) & (i == 0))
    def _():
        base_scr[...] = jnp.zeros_like(base_scr)

    o_refs, l_refs = (o0_ref, o1_ref, o2_ref), (l0_ref, l1_ref, l2_ref)
    outs, lses = [], []
    for gi, (_, r) in enumerate(ATTN_GROUPS):
        lses.append(_token_major(l_scr.at[gi], l_refs[gi][0], tm, r, mid=l_scr.at[0]))
        outs.append(_token_major(o_scr.at[gi], o_refs[gi][0].astype(F32), tm, r, mid=o_scr.at[0]))
    _, _, gate_a = _split_mod(mod_a_ref, b, d)
    shift, scale, _ = _split_mod(mod_m_ref, b, d)
    lane = lax.broadcasted_iota(jnp.int32, (1, LANES), 1)
    n_slab = d // LANES
    rc = tm // ROUTE_ROW_CHUNKS
    ri = lax.broadcasted_iota(jnp.int32, (rc, rc), 0)
    ci = lax.broadcasted_iota(jnp.int32, (rc, rc), 1)
    before = jnp.where(ci < ri, 1.0, 0.0).astype(BF16)
    count = base_scr[0:1, :]

    for c in range(ROUTE_ROW_CHUNKS):
        rows = slice(c * rc, (c + 1) * rc)
        ls = [l[rows] for l in lses]
        m = jnp.maximum(jnp.maximum(ls[0], ls[1]), ls[2])
        es = [jnp.exp(l - m) for l in ls]
        inv = 1.0 / (es[0] + es[1] + es[2])
        scaled = []
        for gi in range(len(ATTN_GROUPS)):
            alpha = es[gi] * inv
            hi = alpha.astype(BF16)
            lo = (alpha - hi.astype(F32)).astype(BF16)
            wide = jnp.dot(hi, expand_ref[...], preferred_element_type=F32) + jnp.dot(
                lo, expand_ref[...], preferred_element_type=F32)
            scaled.append((outs[gi][rows] * wide).astype(BF16))
        attn = jnp.dot(jnp.concatenate(scaled, axis=1), wo_ref[...], preferred_element_type=F32)
        x = x_ref[0, rows, :] + gate_a * attn
        x_out_ref[0, rows, :] = x

        h = _norm_mod(x, g_ref[...], scale, shift)
        for k in range(n_slab):
            h_out_ref[pl.ds(c * rc * n_slab + k, rc, stride=n_slab), :] = h[:, k * LANES:(k + 1) * LANES]

        h_hi = h.astype(BF16)
        h_lo = (h - h_hi.astype(F32)).astype(BF16)
        logits = (jnp.dot(h_hi, wr_ref[0], preferred_element_type=F32)
                  + jnp.dot(h_lo, wr_ref[0], preferred_element_type=F32)
                  + jnp.dot(h_hi, wr_ref[1], preferred_element_type=F32))
        logits = jnp.where(lane < n_exp, logits, NEG_INF)
        m1 = jnp.max(logits, axis=-1, keepdims=True)
        i1 = jnp.min(jnp.where(logits == m1, lane, LANES), axis=-1, keepdims=True)
        rest = jnp.where(lane == i1, NEG_INF, logits)
        m2 = jnp.max(rest, axis=-1, keepdims=True)
        i2 = jnp.min(jnp.where(rest == m2, lane, LANES), axis=-1, keepdims=True)
        e2 = jnp.exp(m2 - m1)
        w1 = 1.0 / (1.0 + e2)
        w2 = e2 * w1
        wts_ref[rows, :] = jnp.where(lane == 0, w1, jnp.where(lane == 1, w2, 0.0))

        oh1, oh2 = (lane == i1), (lane == i2)
        picks = jnp.where(oh1 | oh2, 1.0, 0.0).astype(BF16)
        rank = jnp.dot(before, picks, preferred_element_type=F32) + count
        r1 = jnp.sum(jnp.where(oh1, rank, 0.0), axis=-1, keepdims=True).astype(jnp.int32)
        r2 = jnp.sum(jnp.where(oh2, rank, 0.0), axis=-1, keepdims=True).astype(jnp.int32)
        route = jnp.where(lane == 0, r1, jnp.where(lane == 1, r2, jnp.where(
            lane == 2, i1, jnp.where(lane == 3, i2, 0))))
        route_ref[:, rows] = jnp.transpose(route)[:SUBLANES, :]
        count = count + jnp.sum(picks.astype(F32), axis=0, keepdims=True)

    base_scr[...] = jnp.broadcast_to(count, base_scr.shape)
    cnt_ref[...] = jnp.broadcast_to(count, cnt_ref.shape).astype(jnp.int32)


def _attn_out_route(x, mods, set_a, set_m, g, outs, lses, expand, w_o, w_router, tm):
    bsz, s, d = x.shape
    t = bsz * s
    n_i = s // tm
    n_groups = len(ATTN_GROUPS)
    gl = GROUP_LANES
    tok = lambda b, i: (b, i, 0)
    flat = lambda b, i: (b * n_i + i, 0)
    const2 = lambda b, i: (0, 0)
    return pl.pallas_call(
        _attn_out_route_kernel,
        out_shape=[
            jax.ShapeDtypeStruct((bsz, s, d), F32),
            jax.ShapeDtypeStruct((t * ROW_SLAB, LANES), F32),
            jax.ShapeDtypeStruct((SUBLANES, t), jnp.int32),
            jax.ShapeDtypeStruct((t, LANES), F32),
            jax.ShapeDtypeStruct((SUBLANES, LANES), jnp.int32),
        ],
        grid=(bsz, n_i),
        in_specs=[
            pl.BlockSpec((1, tm, d), tok),
            pl.BlockSpec((1, SUBLANES, 3 * d), lambda b, i: (set_a, 0, 0)),
            pl.BlockSpec((1, SUBLANES, 3 * d), lambda b, i: (set_m, 0, 0)),
            pl.BlockSpec((1, d), const2),
        ] + [pl.BlockSpec((1, tm, gl), tok)] * n_groups
          + [pl.BlockSpec((1, tm, LANES), tok)] * n_groups + [
            pl.BlockSpec((LANES, gl), const2),
            pl.BlockSpec((n_groups * gl, d), const2),
            pl.BlockSpec((2, d, LANES), lambda b, i: (0, 0, 0)),
        ],
        out_specs=[
            pl.BlockSpec((1, tm, d), tok),
            pl.BlockSpec((tm * ROW_SLAB, LANES), flat),
            pl.BlockSpec((SUBLANES, tm), lambda b, i: (0, b * n_i + i)),
            pl.BlockSpec((tm, LANES), flat),
            pl.BlockSpec((SUBLANES, LANES), const2),
        ],
        scratch_shapes=[pltpu.VMEM((n_groups, gl // LANES, tm, LANES), F32),
                        pltpu.VMEM((n_groups, 1, tm, LANES), F32),
                        pltpu.VMEM((SUBLANES, LANES), F32)],
        compiler_params=_cparams(("arbitrary", "arbitrary")),
        name="attn_out_route",
    )(x, mods, mods, g, *outs, *lses, expand, w_o, w_router)


DMA_LOOP_UNROLL = 16


def _for_each(n, fn):
    def body(g, carry):
        for u in range(DMA_LOOP_UNROLL):
            fn(g * DMA_LOOP_UNROLL + u)
        return carry
    lax.fori_loop(0, n // DMA_LOOP_UNROLL, body, 0)


def _row_copy(src, src_row, dst, dst_row, sem):
    return pltpu.make_async_copy(src.at[pl.ds(pl.multiple_of(src_row * ROW_SLAB, ROW_SLAB), ROW_SLAB)],
                                 dst.at[pl.ds(pl.multiple_of(dst_row * ROW_SLAB, ROW_SLAB), ROW_SLAB)],
                                 sem)


def _dispatch_kernel(fill_ref, pos_ref, h_ref, xs_hbm, zeros_scr, sem, zsem, *, tmd, te, n_exp,
                     n_tiles):
    step = pl.program_id(0)

    def zero_fills():
        def zero_copy(first_row, n):
            start = pl.multiple_of(first_row * ROW_SLAB, ROW_SLAB)
            return pltpu.make_async_copy(zeros_scr.at[pl.ds(0, n * ROW_SLAB)],
                                         xs_hbm.at[pl.ds(start, n * ROW_SLAB)], zsem)
        fills = []
        for e in range(n_exp):
            row, length = fill_ref[e], fill_ref[n_exp + e]
            for bit in reversed(range(te.bit_length() - 1)):
                take = (length & (1 << bit)) != 0
                fills.append((take, zero_copy(row, 1 << bit)))
                row = row + jnp.where(take, 1 << bit, 0)
        n_used = fill_ref[2 * n_exp]
        for k in range(n_exp):
            fills.append((n_used + k < n_tiles, zero_copy((n_used + k) * te, te)))
        return fills

    @pl.when(step == 0)
    def _():
        zeros_scr[...] = jnp.zeros_like(zeros_scr)
        for take, cp in zero_fills():
            pl.when(take)(cp.start)
        for take, cp in zero_fills():
            pl.when(take)(cp.wait)

    def copies(tok):
        return [_row_copy(h_ref, tok, xs_hbm, pos_ref[0, 0, k * tmd + tok], sem)
                for k in range(TOP_K)]

    _for_each(tmd, lambda tok: [cp.start(priority=k) for k, cp in enumerate(copies(tok))])
    _for_each(tmd, lambda tok: [cp.wait() for cp in copies(tok)])


def _moe_dispatch(fill, pos_blocks, h_rows, n_tiles, n_exp, tmd, te):
    n_steps = pos_blocks.shape[0]
    assert te & (te - 1) == 0
    kernel = functools.partial(_dispatch_kernel, tmd=tmd, te=te, n_exp=n_exp, n_tiles=n_tiles)
    return pl.pallas_call(
        kernel,
        out_shape=jax.ShapeDtypeStruct((n_tiles * te * ROW_SLAB, LANES), F32),
        grid_spec=pltpu.PrefetchScalarGridSpec(
            num_scalar_prefetch=1,
            grid=(n_steps,),
            in_specs=[
                pl.BlockSpec((1, 1, tmd * TOP_K), lambda s, cnt: (s, 0, 0), memory_space=pltpu.SMEM),
                pl.BlockSpec((tmd * ROW_SLAB, LANES), lambda s, cnt: (s, 0)),
            ],
            out_specs=pl.BlockSpec(memory_space=pl.ANY),
            scratch_shapes=[pltpu.VMEM((te * ROW_SLAB, LANES), F32), pltpu.SemaphoreType.DMA,
                            pltpu.SemaphoreType.DMA],
        ),
        compiler_params=_cparams(("arbitrary",)),
        name="moe_dispatch",
    )(fill, pos_blocks, h_rows)


def _slab_rows_to_matrix(ref, n_rows, n_chunks):
    return jnp.concatenate([ref[pl.ds(c, n_rows, stride=n_chunks), :] for c in range(n_chunks)],
                           axis=1)


def _experts_kernel(tile_expert_ref, n_used_ref, x_ref, wg_ref, wu_ref, wd_ref, y_ref, *, te,
                    f_chunks):
    del tile_expert_ref
    used = pl.program_id(0) < n_used_ref[0]

    @pl.when(jnp.logical_not(used))
    def _():
        y_ref[...] = jnp.zeros_like(y_ref)

    @pl.when(used)
    def _():
        d = wg_ref.shape[1]
        n_chunks = d // LANES
        x = _slab_rows_to_matrix(x_ref, te, n_chunks).astype(BF16)
        y = jnp.zeros((te, d), F32)
        for lo, hi in f_chunks:
            gate = jnp.dot(x, wg_ref[0, :, lo:hi], preferred_element_type=F32)
            up = jnp.dot(x, wu_ref[0, :, lo:hi], preferred_element_type=F32)
            y = y + jnp.dot((_silu(gate) * up).astype(BF16), wd_ref[0, lo:hi, :],
                            preferred_element_type=F32)
        for c in range(n_chunks):
            y_ref[pl.ds(c, te, stride=n_chunks), :] = y[:, c * LANES:(c + 1) * LANES]


def _f_chunks(f):
    if f <= 1536:
        return ((0, f),)
    half = (f // 2 + 255) // 256 * 256
    return ((0, half), (half, f))


def _moe_experts(tile_expert, n_used, xs, w_gate, w_up, w_down, te):
    n_tiles = tile_expert.shape[0]
    _, d, f = w_gate.shape
    kernel = functools.partial(_experts_kernel, te=te, f_chunks=_f_chunks(f))
    weights = lambda i, te_, nu_: (te_[i], 0, 0)
    return pl.pallas_call(
        kernel,
        out_shape=jax.ShapeDtypeStruct(xs.shape, F32),
        grid_spec=pltpu.PrefetchScalarGridSpec(
            num_scalar_prefetch=2,
            grid=(n_tiles,),
            in_specs=[
                pl.BlockSpec((te * ROW_SLAB, LANES), lambda i, te_, nu_: (jnp.minimum(i, nu_[0] - 1), 0)),
                pl.BlockSpec((1, d, f), weights),
                pl.BlockSpec((1, d, f), weights),
                pl.BlockSpec((1, f, d), weights),
            ],
            out_specs=pl.BlockSpec((te * ROW_SLAB, LANES), lambda i, te_, nu_: (i, 0)),
        ),
        compiler_params=_cparams(("arbitrary",)),
        name="moe_experts",
    )(tile_expert, n_used, xs, w_gate, w_up, w_down)


def _combine_kernel(pos_ref, pos_next_ref, x_ref, mod_ref, g_ref, wts_ref, ys_hbm, o_ref, buf, sems,
                    *, tmc):
    b = pl.program_id(0)
    step = b * pl.num_programs(1) + pl.program_id(1)
    n_steps = pl.num_programs(0) * pl.num_programs(1)
    slot = step % 2
    d = x_ref.shape[2]
    n_chunks = d // LANES

    def copies(p_ref, slot_, tok):
        return [_row_copy(ys_hbm, p_ref[0, 0, k * tmc + tok], buf.at[slot_], k * tmc + tok,
                          sems.at[slot_]) for k in range(TOP_K)]

    def fetch(p_ref, slot_):
        _for_each(tmc, lambda tok: [cp.start(priority=k)
                                    for k, cp in enumerate(copies(p_ref, slot_, tok))])

    pl.when(step == 0)(lambda: fetch(pos_ref, slot))
    pl.when(step + 1 < n_steps)(lambda: fetch(pos_next_ref, 1 - slot))
    _for_each(tmc, lambda tok: [cp.wait() for cp in copies(pos_ref, slot, tok)])

    wts = wts_ref[...]
    y = jnp.zeros((tmc, d), F32)
    rows = buf.at[slot]
    for k in range(TOP_K):
        yk = _slab_rows_to_matrix(rows.at[pl.ds(k * tmc * ROW_SLAB, tmc * ROW_SLAB)], tmc, n_chunks)
        y = y + wts[:, k:k + 1] * yk
    _, _, gate = _split_mod(mod_ref, b, d)
    x = x_ref[0] + gate * y
    o_ref[0] = x * lax.rsqrt(jnp.mean(x * x, axis=-1, keepdims=True) + RMS_EPS) * g_ref[...]


def _moe_combine(pos_blocks, x, mods, set_idx, final_g, wts, ys, tmc):
    bsz, s, d = x.shape
    n_i = s // tmc
    last = bsz * n_i - 1
    kernel = functools.partial(_combine_kernel, tmc=tmc)
    return pl.pallas_call(
        kernel,
        out_shape=jax.ShapeDtypeStruct(x.shape, F32),
        grid=(bsz, n_i),
        in_specs=[
            pl.BlockSpec((1, 1, tmc * TOP_K), lambda b, i: (b * n_i + i, 0, 0),
                         memory_space=pltpu.SMEM),
            pl.BlockSpec((1, 1, tmc * TOP_K), lambda b, i: (jnp.minimum(b * n_i + i + 1, last), 0, 0),
                         memory_space=pltpu.SMEM),
            pl.BlockSpec((1, tmc, d), lambda b, i: (b, i, 0)),
            pl.BlockSpec((1, SUBLANES, 3 * d), lambda b, i: (set_idx, 0, 0)),
            pl.BlockSpec((1, d), lambda b, i: (0, 0)),
            pl.BlockSpec((tmc, LANES), lambda b, i: (b * n_i + i, 0)),
            pl.BlockSpec(memory_space=pl.ANY),
        ],
        out_specs=pl.BlockSpec((1, tmc, d), lambda b, i: (b, i, 0)),
        scratch_shapes=[pltpu.VMEM((2, TOP_K * tmc * ROW_SLAB, LANES), F32),
                        pltpu.SemaphoreType.DMA((2,))],
        compiler_params=_cparams(("arbitrary", "arbitrary")),
        name="moe_combine",
    )(pos_blocks, pos_blocks, x, mods, final_g, wts, ys)


def _pad_group_columns(w, which):
    width = HEADS_PER_GROUP * HEAD_DIM
    n_groups = len(ATTN_GROUPS)
    cols = w[:, which * n_groups * width:(which + 1) * n_groups * width]
    cols = cols.reshape(w.shape[0], n_groups, width)
    return jnp.pad(cols, ((0, 0), (0, 0), (0, GROUP_LANES - width)))


def kernel(x, c, positions, mod_w, mod_b, norm_g, conv_w_in, conv_w, conv_w_out, ffn_w_gate,
           ffn_w_up, ffn_w_down, attn_w_qkv, attn_w_o, router_w, moe_w_gate, moe_w_up, moe_w_down,
           final_g):
    bsz, s, d = x.shape
    t = bsz * s
    n_groups = len(ATTN_GROUPS)
    n_exp = router_w.shape[-1]
    assert all(w // r == ATTN_STEPS for w, r in ATTN_GROUPS) and ATTN_GROUPS[0][1] == 1
    assert bsz <= SUBLANES and n_exp <= SUBLANES and d == ROW_SLAB * LANES
    tm = min(512, s)
    assert s % tm == 0 and all(s % (r * ATTN_STEPS) == 0 and tm % r == 0 for _, r in ATTN_GROUPS)

    c_pad = jnp.pad(c.astype(F32), ((0, SUBLANES - bsz), (0, 0)))
    mods = _ada_params(c_pad, mod_w.reshape(-1, d, 3 * d), mod_b.reshape(-1, 1, 3 * d))

    x, e_gate, e_up, e_down = _conv_mixer(
        x, mods, 0, norm_g[0, 0][None], conv_w_in[0].astype(BF16), conv_w[0],
        conv_w_out[0].astype(BF16), moe_w_gate[0], moe_w_up[0], moe_w_down[0], tm)
    x = _dense_ffn(x, mods, 1, norm_g[0, 1][None], ffn_w_gate[0].astype(BF16),
                   ffn_w_up[0].astype(BF16), ffn_w_down[0].astype(BF16), tm)

    w_qkv = attn_w_qkv[0]
    w_groups = jnp.concatenate([_pad_group_columns(w_qkv, which) for which in range(3)], axis=2)
    w_groups = w_groups.transpose(1, 0, 2).astype(BF16)
    inv_freq = ROPE_THETA ** (-jnp.arange(0, HEAD_DIM, 2, dtype=F32) / HEAD_DIM)
    freq = jnp.tile(inv_freq, LANES // (HEAD_DIM // 2))[None]
    qkvs = _qkv_rope(x, mods, 2, norm_g[1, 0][None], positions[..., None], freq, w_groups, tm)
    outs, lses = zip(*[_dilated_attn(qkv, r, tm) for qkv, (_, r) in zip(qkvs, ATTN_GROUPS)])

    width = HEADS_PER_GROUP * HEAD_DIM
    w_o = jnp.pad(attn_w_o[0].reshape(n_groups, width, d), ((0, 0), (0, GROUP_LANES - width), (0, 0)))
    w_o = w_o.reshape(n_groups * GROUP_LANES, d).astype(BF16)
    expand = (jnp.arange(GROUP_LANES)[None, :] // HEAD_DIM == jnp.arange(LANES)[:, None]).astype(BF16)
    w_router = jnp.pad(router_w[0], ((0, 0), (0, LANES - n_exp)))
    w_router_hi = w_router.astype(BF16)
    w_router = jnp.stack([w_router_hi, (w_router - w_router_hi.astype(F32)).astype(BF16)])

    x, h_rows, route, wts, counts = _attn_out_route(
        x, mods, 2, 3, norm_g[1, 1][None], outs, lses, expand, w_o, w_router, tm)

    te = min(512, t)
    n_tiles = (t * TOP_K) // te + n_exp
    counts = counts[0, :n_exp]
    tiles_per_expert = (counts + te - 1) // te
    ends = jnp.cumsum(tiles_per_expert)
    first_row = (ends - tiles_per_expert) * te
    n_used = ends[-1:]
    experts = route[TOP_K:2 * TOP_K]
    pos = route[:TOP_K] + sum(jnp.where(experts == e, first_row[e], 0) for e in range(n_exp))
    fill = jnp.concatenate([first_row + counts, tiles_per_expert * te - counts, n_used]).astype(jnp.int32)
    idx = jnp.minimum(jnp.arange(n_tiles, dtype=jnp.int32), n_used - 1)
    tile_expert = jnp.sum(idx[:, None] >= ends[None, :], axis=1).astype(jnp.int32)

    pos_blocks = pos.reshape(TOP_K, t // tm, tm).transpose(1, 0, 2).reshape(t // tm, 1, TOP_K * tm)
    xs = _moe_dispatch(fill, pos_blocks, h_rows, n_tiles, n_exp, tm, te)
    ys = _moe_experts(tile_expert, n_used.astype(jnp.int32), xs, e_gate, e_up, e_down, te)
    return _moe_combine(pos_blocks, x, mods, 3, final_g[None], wts, ys, tm)
```
